```python
import jax, jax.numpy as jnp
from jax import lax
import numpy as np

D_MODEL = 2048
BATCH = 2
SEQ = 4096
DEPTH = 4
DEC_BATCH = 8
DEC_SEQ = 8
PAST_LEN = 16384
PAGE_SIZE = 128

H_A = 6
HD_A = 128
D_A = H_A * HD_A
DILATED_PATTERNS = ((128, 1), (512, 4), (2048, 16))
WIN_MAX = 2048
POOL_WINDOWS = (2, 4, 8, 16)
N_POOL = 4
D_B = 512
POOL_GROUP = D_B // N_POOL
POOL_BUF = 15
H_C = 4
DK_C = 96
DV_C = 192
D_C = H_C * DV_C
GATE_RANK = 16
GATE_NORMALIZER = 16.0
GLA_CHUNK = 32
PROJ_SPLITS = (D_A, D_A, D_A, D_B, H_C * DK_C, H_C * DK_C, D_C, D_C, GATE_RANK)
D_IN = 3 * D_A + D_B + 2 * H_C * DK_C + 2 * D_C + GATE_RANK
D_MIX = D_A + D_B + D_C
D_FF = -(-8 * D_MODEL // (3 * 256)) * 256
EPS = 1e-6
NEG_INF = -1e30

kernel_name = 'hybrid_dilated_pool_gla_step'


def rmsnorm(x, g):
    xf = x.astype(jnp.float32)
    y = xf * lax.rsqrt(jnp.mean(xf * xf, axis=-1, keepdims=True) + EPS)
    return (y * g.astype(jnp.float32)).astype(x.dtype)


def dilated_band(q, k, v, window, dil):
    B, S, H, Dh = q.shape
    blk = window // dil
    span = dil * blk
    S_pad = -(-S // span) * span
    L = S_pad // dil
    nb = L // blk

    def to_blocks(t):
        t = jnp.pad(t.astype(jnp.float32), ((0, 0), (0, S_pad - S), (0, 0), (0, 0)))
        t = jnp.moveaxis(t.reshape(B, L, dil, H, Dh), 2, 1)
        return t.reshape(B, dil, nb, blk, H, Dh)

    def with_prev(t):
        prev = jnp.pad(t[:, :, :-1], ((0, 0), (0, 0), (1, 0), (0, 0), (0, 0), (0, 0)))
        return jnp.concatenate([prev, t], axis=3)

    qb, kb, vb = to_blocks(q), to_blocks(k), to_blocks(v)
    kk, vv = with_prev(kb), with_prev(vb)
    s = jnp.einsum('brnqhd,brnkhd->brnhqk', qb, kk) * (Dh ** -0.5)
    qi = jnp.arange(blk)[:, None]
    ki = jnp.arange(2 * blk)[None, :]
    dist = qi + blk - ki
    band = (dist >= 0) & (dist <= blk)
    exists = (jnp.arange(nb)[:, None, None] > 0) | (ki[None] >= blk)
    mask = band[None] & exists
    s = jnp.where(mask[None, None, :, None], s, NEG_INF)
    m = jnp.max(s, axis=-1, keepdims=True)
    p = jnp.exp(s - m)
    den = jnp.sum(p, axis=-1, keepdims=True)
    o = jnp.einsum('brnhqk,brnkhd->brnqhd', p, vv) / jnp.transpose(den, (0, 1, 2, 4, 3, 5))
    lse = jnp.transpose((m + jnp.log(den))[..., 0], (0, 1, 2, 4, 3))

    def from_blocks(t):
        t = t.reshape((B, dil, L) + t.shape[4:])
        t = jnp.moveaxis(t, 1, 2).reshape((B, S_pad) + t.shape[3:])
        return t[:, :S]

    return from_blocks(o), from_blocks(lse)


def dilated_gather(q, k_ext, v_ext, window, dil, P):
    T = q.shape[1]
    Dh = q.shape[-1]
    n = window // dil
    idx = P + jnp.arange(T)[:, None] - dil * jnp.arange(n + 1)[None, :]
    valid = idx >= 0
    idxc = jnp.maximum(idx, 0)
    kg = k_ext[:, idxc].astype(jnp.float32)
    vg = v_ext[:, idxc].astype(jnp.float32)
    s = jnp.einsum('bthd,btjhd->bthj', q.astype(jnp.float32), kg) * (Dh ** -0.5)
    s = jnp.where(valid[None, :, None, :], s, NEG_INF)
    m = jnp.max(s, axis=-1, keepdims=True)
    p = jnp.exp(s - m)
    den = jnp.sum(p, axis=-1, keepdims=True)
    o = jnp.einsum('bthj,btjhd->bthd', p, vg) / den
    return o, (m + jnp.log(den))[..., 0]


def pool_mix(u_ext, pos0, w_pool, scale):
    B, Lx, C = u_ext.shape
    T = Lx - POOL_BUF
    uf = u_ext.astype(jnp.float32)
    csum = jnp.concatenate([jnp.zeros((B, 1, C), jnp.float32), jnp.cumsum(uf, axis=1)], axis=1)
    end = csum[:, POOL_BUF + 1:]
    pos = pos0 + jnp.arange(T)
    means = []
    for gi, w in enumerate(POOL_WINDOWS):
        lo, hi = gi * POOL_GROUP, (gi + 1) * POOL_GROUP
        start = csum[:, POOL_BUF + 1 - w: POOL_BUF + 1 - w + T, lo:hi]
        cnt = jnp.minimum(w, pos + 1).astype(jnp.float32)[None, :, None]
        means.append((end[..., lo:hi] - start) / cnt)
    pooled = (jnp.concatenate(means, axis=-1) - uf[:, POOL_BUF:]).reshape(B, T, N_POOL, POOL_GROUP)
    out = jnp.einsum('btgc,gcd->btgd', pooled, w_pool.astype(jnp.float32)).reshape(B, T, D_B)
    return (out * scale.astype(jnp.float32)).astype(u_ext.dtype)


def gla_chunked(q, k, v, g, S0):
    B, T, H, DK = q.shape
    DV = v.shape[-1]
    c = min(GLA_CHUNK, T)
    Tp = -(-T // c) * c
    n = Tp // c

    def prep(t):
        t = jnp.pad(t.astype(jnp.float32), ((0, 0), (0, Tp - T), (0, 0), (0, 0)))
        return jnp.transpose(t.reshape(B, n, c, H, t.shape[-1]), (1, 0, 3, 2, 4))

    tril = jnp.tril(jnp.ones((c, c), dtype=bool))

    def step(S, inp):
        qc, kc, vc, gc = inp
        b = jnp.cumsum(gc, axis=2)
        qe = qc * jnp.exp(b)
        ke = kc * jnp.exp(-b)
        A = jnp.where(tril, jnp.einsum('bhid,bhjd->bhij', qe, ke), 0.0)
        o = jnp.einsum('bhij,bhjv->bhiv', A, vc) + jnp.einsum('bhid,bhdv->bhiv', qe, S)
        bl = b[:, :, -1:, :]
        S = jnp.exp(bl)[:, :, 0, :, None] * S + jnp.einsum('bhjd,bhjv->bhdv', kc * jnp.exp(bl - b), vc)
        return S, o

    S, o = lax.scan(step, S0.astype(jnp.float32), (prep(q), prep(k), prep(v), prep(g)))
    o = jnp.transpose(o, (1, 0, 3, 2, 4)).reshape(B, Tp, H, DV)[:, :T]
    return o, S


def layer(x, kv_buf, pool_buf, gla_s0, pos0, norm1_g, w_in, w_gate2, b_gate, w_pool, pool_scale,
          gla_norm_g, w_o, norm2_g, w_ffn_gate, w_ffn_up, w_ffn_down):
    B, T, _ = x.shape
    h = rmsnorm(x, norm1_g)
    proj = jnp.einsum('btd,de->bte', h, w_in)
    points = [int(p) for p in np.cumsum(PROJ_SPLITS)[:-1]]
    qa, ka, va, ub, qc, kc, vc, rc, ga = jnp.split(proj, points, axis=-1)
    qa = qa.reshape(B, T, H_A, HD_A)
    ka = ka.reshape(B, T, H_A, HD_A)
    va = va.reshape(B, T, H_A, HD_A)
    if kv_buf is None:
        outs = [dilated_band(qa, ka, va, w, d) for (w, d) in DILATED_PATTERNS]
    else:
        k_buf, v_buf = kv_buf
        P = k_buf.shape[1]
        k_ext = jnp.concatenate([k_buf.astype(ka.dtype), ka], axis=1)
        v_ext = jnp.concatenate([v_buf.astype(va.dtype), va], axis=1)
        outs = [dilated_gather(qa, k_ext, v_ext, w, d, P) for (w, d) in DILATED_PATTERNS]
    o_stack = jnp.stack([o for o, _ in outs])
    lse_stack = jnp.stack([l for _, l in outs])
    wts = jax.nn.softmax(lse_stack, axis=0)
    out_a = jnp.sum(wts[..., None] * o_stack, axis=0).reshape(B, T, D_A).astype(x.dtype)
    if pool_buf is None:
        pool_buf = jnp.zeros((B, POOL_BUF, D_B), ub.dtype)
    u_ext = jnp.concatenate([pool_buf.astype(ub.dtype), ub], axis=1)
    out_b = pool_mix(u_ext, pos0, w_pool, pool_scale).astype(x.dtype)
    new_pool = u_ext[:, -POOL_BUF:]
    qc = qc.reshape(B, T, H_C, DK_C) * (DK_C ** -0.5)
    kc = kc.reshape(B, T, H_C, DK_C)
    vc = vc.reshape(B, T, H_C, DV_C)
    g = jax.nn.log_sigmoid(jnp.einsum('btr,re->bte', ga, w_gate2).astype(jnp.float32)
                           + b_gate.astype(jnp.float32)) / GATE_NORMALIZER
    g = g.reshape(B, T, H_C, DK_C)
    oc, S_new = gla_chunked(qc, kc, vc, g, gla_s0)
    oc = oc * lax.rsqrt(jnp.mean(oc * oc, axis=-1, keepdims=True) + EPS) * gla_norm_g.astype(jnp.float32)
    out_c = (oc.reshape(B, T, D_C) * jax.nn.silu(rc.astype(jnp.float32))).astype(x.dtype)
    mix = jnp.concatenate([out_a, out_b, out_c], axis=-1)
    x = x + jnp.einsum('btm,md->btd', mix, w_o)
    h2 = rmsnorm(x, norm2_g)
    f = jax.nn.silu(jnp.einsum('btd,df->btf', h2, w_ffn_gate)) * jnp.einsum('btd,df->btf', h2, w_ffn_up)
    x = x + jnp.einsum('btf,fd->btd', f, w_ffn_down)
    return x, ka, va, new_pool, S_new.astype(x.dtype)


def setup_inputs(seed: int = 0) -> dict:
    key = jax.random.key(seed)
    ks = jax.random.split(key, 20)
    f32 = jnp.float32

    def nrm(k, shape, scale):
        return jax.random.normal(k, shape, f32) * scale

    w_buf = min(WIN_MAX, PAST_LEN)
    return {
        'x_prompt': nrm(ks[0], (BATCH, SEQ, D_MODEL), 1.0),
        'x_sample': nrm(ks[1], (DEC_BATCH, DEC_SEQ, D_MODEL), 1.0),
        'cache_win_k': nrm(ks[2], (DEPTH, DEC_BATCH, w_buf, H_A, HD_A), 1.0),
        'cache_win_v': nrm(ks[3], (DEPTH, DEC_BATCH, w_buf, H_A, HD_A), 1.0),
        'state_pool': nrm(ks[4], (DEPTH, DEC_BATCH, POOL_BUF, D_B), 1.0),
        'state_gla': nrm(ks[5], (DEPTH, DEC_BATCH, H_C, DK_C, DV_C), 0.3),
        'norm1_g': 1.0 + nrm(ks[6], (DEPTH, D_MODEL), 0.02),
        'w_in': nrm(ks[7], (DEPTH, D_MODEL, D_IN), D_MODEL ** -0.5),
        'w_gate2': nrm(ks[8], (DEPTH, GATE_RANK, H_C * DK_C), GATE_RANK ** -0.5),
        'b_gate': nrm(ks[9], (DEPTH, H_C * DK_C), 0.1),
        'w_pool': nrm(ks[10], (DEPTH, N_POOL, POOL_GROUP, POOL_GROUP), POOL_GROUP ** -0.5),
        'pool_scale': 1.0 + nrm(ks[11], (DEPTH, D_B), 0.02),
        'gla_norm_g': 1.0 + nrm(ks[12], (DEPTH, DV_C), 0.02),
        'w_o': nrm(ks[13], (DEPTH, D_MIX, D_MODEL), D_MIX ** -0.5),
        'norm2_g': 1.0 + nrm(ks[14], (DEPTH, D_MODEL), 0.02),
        'w_ffn_gate': nrm(ks[15], (DEPTH, D_MODEL, D_FF), D_MODEL ** -0.5),
        'w_ffn_up': nrm(ks[16], (DEPTH, D_MODEL, D_FF), D_MODEL ** -0.5),
        'w_ffn_down': nrm(ks[17], (DEPTH, D_FF, D_MODEL), D_FF ** -0.5),
        'final_norm_g': 1.0 + nrm(ks[18], (D_MODEL,), 0.02),
    }


def reference(x_prompt, x_sample, cache_win_k, cache_win_v, state_pool, state_gla, norm1_g, w_in,
              w_gate2, b_gate, w_pool, pool_scale, gla_norm_g, w_o, norm2_g, w_ffn_gate, w_ffn_up,
              w_ffn_down, final_norm_g):
    xp, xs = x_prompt, x_sample
    Bp = xp.shape[0]
    keep = min(WIN_MAX, xp.shape[1])
    p_k, p_v, p_pool, p_gla = [], [], [], []
    s_k, s_v, s_pool, s_gla = [], [], [], []
    for l in range(DEPTH):
        wl = (norm1_g[l], w_in[l], w_gate2[l], b_gate[l], w_pool[l], pool_scale[l], gla_norm_g[l],
              w_o[l], norm2_g[l], w_ffn_gate[l], w_ffn_up[l], w_ffn_down[l])
        s0 = jnp.zeros((Bp, H_C, DK_C, DV_C), jnp.float32)
        xp, ka, va, pool_new, S = layer(xp, None, None, s0, 0, *wl)
        p_k.append(ka[:, -keep:])
        p_v.append(va[:, -keep:])
        p_pool.append(pool_new)
        p_gla.append(S)
        xs, ka, va, pool_new, S = layer(xs, (cache_win_k[l], cache_win_v[l]), state_pool[l], state_gla[l],
                                        PAST_LEN, *wl)
        s_k.append(ka)
        s_v.append(va)
        s_pool.append(pool_new)
        s_gla.append(S)
    y_prompt = rmsnorm(xp, final_norm_g)
    y_sample = rmsnorm(xs, final_norm_g)
    return (y_prompt, y_sample, jnp.stack(p_k), jnp.stack(p_v), jnp.stack(p_pool), jnp.stack(p_gla),
            jnp.stack(s_k), jnp.stack(s_v), jnp.stack(s_pool), jnp.stack(s_gla))
```

```python
import functools
import math

import jax
import jax.numpy as jnp
from jax import lax
from jax.experimental import pallas as pl
from jax.experimental.pallas import tpu as pltpu

F32 = jnp.float32
BF16 = jnp.bfloat16

H_A, HD_A = 6, 128
D_A = H_A * HD_A
DILATED_PATTERNS = ((128, 1), (512, 4), (2048, 16))
POOL_WINDOWS = (2, 4, 8, 16)
N_POOL = 4
D_B = 512
POOL_GROUP = D_B // N_POOL
POOL_BUF = 15
H_C, DK_C, DV_C = 4, 96, 192
D_C = H_C * DV_C
GATE_RANK = 16
GATE_NORMALIZER = 16.0
GLA_CHUNK = 32
PAST_LEN = 16384
EPS = 1e-6
NEG_INF = -1e30

LANE = 128
SUBLANE = 8
MXU_N = 256
VMEM_LIMIT_BYTES = 56 * 1024 * 1024

DK_P = 128
DV_P = 256
GATE_P = 256
C_Q0 = 0
C_K0 = C_Q0 + H_C * DK_P
C_V0 = C_K0 + H_C * DK_P
C_R0 = C_V0 + H_C * DV_P
C_G0 = C_R0 + H_C * DV_P
C_COLS = C_G0 + GATE_P
HIST_ROWS = 16


def _params(*sem):
    return pltpu.CompilerParams(dimension_semantics=sem, vmem_limit_bytes=VMEM_LIMIT_BYTES)


def _row_tile(m, target):
    if m <= target:
        return m
    t = target
    while t >= SUBLANE:
        if m % t == 0 and t % SUBLANE == 0:
            return t
        t -= SUBLANE
    raise ValueError(f"no row tile for {m}")


def _rmsnorm_kernel(x_ref, g_ref, o_ref):
    x = x_ref[...]
    ms = jnp.mean(x * x, axis=-1, keepdims=True)
    o_ref[...] = (x * lax.rsqrt(ms + EPS) * g_ref[...]).astype(o_ref.dtype)


def _rmsnorm(x, g, out_dtype):
    m, d = x.shape
    tm = _row_tile(m, 512)
    return pl.pallas_call(
        _rmsnorm_kernel,
        grid=(m // tm,),
        in_specs=[pl.BlockSpec((tm, d), lambda i: (i, 0)), pl.BlockSpec((1, d), lambda i: (0, 0))],
        out_specs=pl.BlockSpec((tm, d), lambda i: (i, 0)),
        out_shape=jax.ShapeDtypeStruct((m, d), out_dtype),
        compiler_params=_params("arbitrary"),
        name="rmsnorm",
    )(x, g.reshape(1, d))


def _mm_kernel(*refs, n_in, has_res):
    xs = refs[:n_in]
    ws = refs[n_in:2 * n_in]
    pos = 2 * n_in
    res_ref = refs[pos] if has_res else None
    pos += int(has_res)
    o_ref = refs[pos]
    wbs = refs[pos + 1:]

    @pl.when(pl.program_id(1) == 0)
    def _():
        for w_ref, wb_ref in zip(ws, wbs):
            wb_ref[...] = w_ref[...].astype(BF16)

    acc = None
    for x_ref, wb_ref in zip(xs, wbs):
        d = jnp.dot(x_ref[...], wb_ref[...], preferred_element_type=F32)
        acc = d if acc is None else acc + d
    if has_res:
        acc = acc + res_ref[...]
    o_ref[...] = acc.astype(o_ref.dtype)


def _matmul(xs, ws, n, *, res=None, out_dtype=F32, tm_target=1024, tn=MXU_N, name="matmul"):
    m = xs[0].shape[0]
    tm = _row_tile(m, tm_target)
    assert n % tn == 0
    in_specs = []
    for x in xs:
        in_specs.append(pl.BlockSpec((tm, x.shape[1]), lambda j, i: (i, 0)))
    scratch = []
    for (w, k, c0), x in zip(ws, xs):
        assert x.shape[1] == k and c0 % tn == 0
        in_specs.append(pl.BlockSpec((k, tn), lambda j, i, c0=c0: (0, j + c0 // tn)))
        scratch.append(pltpu.VMEM((k, tn), BF16))
    args = list(xs) + [w for w, _, _ in ws]
    if res is not None:
        in_specs.append(pl.BlockSpec((tm, tn), lambda j, i: (i, j)))
        args.append(res)
    return pl.pallas_call(
        functools.partial(_mm_kernel, n_in=len(xs), has_res=res is not None),
        grid=(n // tn, m // tm),
        in_specs=in_specs,
        out_specs=pl.BlockSpec((tm, tn), lambda j, i: (i, j)),
        out_shape=jax.ShapeDtypeStruct((m, n), out_dtype),
        scratch_shapes=scratch,
        compiler_params=_params("arbitrary", "arbitrary"),
        name=name,
    )(*args)


def _glu_kernel(x_ref, wg_ref, wu_ref, o_ref, wgb_ref, wub_ref):
    @pl.when(pl.program_id(1) == 0)
    def _():
        wgb_ref[...] = wg_ref[...].astype(BF16)
        wub_ref[...] = wu_ref[...].astype(BF16)

    x = x_ref[...]
    g = jnp.dot(x, wgb_ref[...], preferred_element_type=F32)
    u = jnp.dot(x, wub_ref[...], preferred_element_type=F32)
    o_ref[...] = (g * (1.0 / (1.0 + jnp.exp(-g))) * u).astype(o_ref.dtype)


def _glu(x, wg, wu, *, tm_target=1024, tn=MXU_N):
    m, k = x.shape
    n = wg.shape[1]
    tm = _row_tile(m, tm_target)
    assert n % tn == 0
    return pl.pallas_call(
        _glu_kernel,
        grid=(n // tn, m // tm),
        in_specs=[
            pl.BlockSpec((tm, k), lambda j, i: (i, 0)),
            pl.BlockSpec((k, tn), lambda j, i: (0, j)),
            pl.BlockSpec((k, tn), lambda j, i: (0, j)),
        ],
        out_specs=pl.BlockSpec((tm, tn), lambda j, i: (i, j)),
        out_shape=jax.ShapeDtypeStruct((m, n), BF16),
        scratch_shapes=[pltpu.VMEM((k, tn), BF16), pltpu.VMEM((k, tn), BF16)],
        compiler_params=_params("arbitrary", "arbitrary"),
        name="ffn_glu",
    )(x, wg, wu)


def _band_attn_kernel(q_ref, kp_ref, kc_ref, vp_ref, vc_ref, o_ref, l_ref):
    n = pl.program_id(2)
    blk = q_ref.shape[1]
    scale = HD_A ** -0.5
    qi = lax.broadcasted_iota(jnp.int32, (blk, blk), 0)
    ki = lax.broadcasted_iota(jnp.int32, (blk, blk), 1)
    mask_prev = ki >= qi + jnp.where(n > 0, 0, blk)
    mask_cur = ki <= qi
    nt = (((1,), (1,)), ((), ()))
    for h in range(H_A):
        sl = slice(h * HD_A, (h + 1) * HD_A)
        q = q_ref[0, :, sl].astype(BF16)
        sp = lax.dot_general(q, kp_ref[0, :, sl].astype(BF16), nt, preferred_element_type=F32) * scale
        sc = lax.dot_general(q, kc_ref[0, :, sl].astype(BF16), nt, preferred_element_type=F32) * scale
        sp = jnp.where(mask_prev, sp, NEG_INF)
        sc = jnp.where(mask_cur, sc, NEG_INF)
        m = jnp.maximum(jnp.max(sp, axis=-1, keepdims=True), jnp.max(sc, axis=-1, keepdims=True))
        pp = jnp.exp(sp - m)
        pc = jnp.exp(sc - m)
        den = jnp.sum(pp, axis=-1, keepdims=True) + jnp.sum(pc, axis=-1, keepdims=True)
        o = jnp.dot(pp.astype(BF16), vp_ref[0, :, sl].astype(BF16), preferred_element_type=F32)
        o = o + jnp.dot(pc.astype(BF16), vc_ref[0, :, sl].astype(BF16), preferred_element_type=F32)
        o_ref[0, :, sl] = o / den
        l_ref[0, :, sl] = jnp.broadcast_to(m + jnp.log(den), (blk, HD_A))


def _band_attn(qkv, batch, seq, window, dil):
    blk = window // dil
    length = seq // dil
    assert seq % window == 0 and blk % SUBLANE == 0
    nb = length // blk
    x = qkv.reshape(batch, length, dil * 3 * D_A)

    def spec(which, prev):
        def index(b, r, n):
            return (b, jnp.maximum(n - 1, 0) if prev else n, 3 * r + which)
        return pl.BlockSpec((1, blk, D_A), index)

    out_spec = pl.BlockSpec((1, blk, D_A), lambda b, r, n: (b, n, r))
    out_shape = jax.ShapeDtypeStruct((batch, length, dil * D_A), F32)
    o, lse = pl.pallas_call(
        _band_attn_kernel,
        grid=(batch, dil, nb),
        in_specs=[spec(0, False), spec(1, True), spec(1, False), spec(2, True), spec(2, False)],
        out_specs=[out_spec, out_spec],
        out_shape=[out_shape, out_shape],
        compiler_params=_params("arbitrary", "arbitrary", "arbitrary"),
        name=f"band_attn_d{dil}",
    )(x, x, x, x, x)
    return o.reshape(batch * seq, D_A), lse.reshape(batch * seq, D_A)


def _combine_kernel(o1, l1, o2, l2, o3, l3, out_ref):
    a1, a2, a3 = l1[...], l2[...], l3[...]
    mx = jnp.maximum(jnp.maximum(a1, a2), a3)
    e1, e2, e3 = jnp.exp(a1 - mx), jnp.exp(a2 - mx), jnp.exp(a3 - mx)
    num = e1 * o1[...] + e2 * o2[...] + e3 * o3[...]
    out_ref[...] = (num / (e1 + e2 + e3)).astype(out_ref.dtype)


def _combine(parts):
    m, d = parts[0][0].shape
    tm = _row_tile(m, 512)
    spec = pl.BlockSpec((tm, d), lambda i: (i, 0))
    flat = [a for pair in parts for a in pair]
    return pl.pallas_call(
        _combine_kernel,
        grid=(m // tm,),
        in_specs=[spec] * 6,
        out_specs=spec,
        out_shape=jax.ShapeDtypeStruct((m, d), BF16),
        compiler_params=_params("arbitrary"),
        name="attn_combine",
    )(*flat)


def _dec_attn_kernel(q_ref, kn_ref, vn_ref, kc_ref, vc_ref, o_ref, *, past):
    t = q_ref.shape[0]
    scale = HD_A ** -0.5
    nt = (((1,), (1,)), ((), ()))
    q = q_ref[...]
    kc = kc_ref[...]
    vc = vc_ref[...]
    kn = kn_ref[...]
    vn = vn_ref[...]
    s_c = lax.dot_general(q, kc, nt, preferred_element_type=F32) * scale
    s_n = lax.dot_general(q, kn, nt, preferred_element_type=F32) * scale
    d_c = past + lax.broadcasted_iota(jnp.int32, (t, past), 0) - lax.broadcasted_iota(jnp.int32, (t, past), 1)
    d_n = lax.broadcasted_iota(jnp.int32, (t, t), 0) - lax.broadcasted_iota(jnp.int32, (t, t), 1)
    outs, lses = [], []
    for window, dil in DILATED_PATTERNS:
        m_c = jnp.logical_and((d_c & (dil - 1)) == 0, d_c <= window)
        m_n = jnp.logical_and(jnp.logical_and(d_n >= 0, (d_n & (dil - 1)) == 0), d_n <= window)
        a_c = jnp.where(m_c, s_c, NEG_INF)
        a_n = jnp.where(m_n, s_n, NEG_INF)
        m = jnp.maximum(jnp.max(a_c, axis=-1, keepdims=True), jnp.max(a_n, axis=-1, keepdims=True))
        p_c = jnp.exp(a_c - m)
        p_n = jnp.exp(a_n - m)
        den = jnp.sum(p_c, axis=-1, keepdims=True) + jnp.sum(p_n, axis=-1, keepdims=True)
        o = jnp.dot(p_c, vc, preferred_element_type=F32) + jnp.dot(p_n, vn, preferred_element_type=F32)
        outs.append(o / den)
        lses.append(m + jnp.log(den))
    mx = jnp.maximum(jnp.maximum(lses[0], lses[1]), lses[2])
    es = [jnp.exp(l - mx) for l in lses]
    num = es[0] * outs[0] + es[1] * outs[1] + es[2] * outs[2]
    o_ref[...] = (num / (es[0] + es[1] + es[2])).astype(o_ref.dtype)


def _dec_attn(qkv, cache_k, cache_v, layer, batch, t):
    past = cache_k.shape[2]
    assert all(d & (d - 1) == 0 for _, d in DILATED_PATTERNS)
    new_spec = lambda which: pl.BlockSpec((t, HD_A), lambda b, h: (b, which * H_A + h))
    cache_spec = pl.BlockSpec((None, None, past, HD_A), lambda b, h: (layer, b, 0, h))
    return pl.pallas_call(
        functools.partial(_dec_attn_kernel, past=past),
        grid=(batch, H_A),
        in_specs=[new_spec(0), new_spec(1), new_spec(2), cache_spec, cache_spec],
        out_specs=pl.BlockSpec((t, HD_A), lambda b, h: (b, h)),
        out_shape=jax.ShapeDtypeStruct((batch * t, D_A), F32),
        compiler_params=_params("arbitrary", "arbitrary"),
        name="dec_attn",
    )(qkv, qkv, qkv, cache_k, cache_v).astype(BF16)


def _pool_kernel(*refs, pos0, has_prev):
    if has_prev:
        u_ref, prev_ref, hist_ref, w_ref, sc_ref, o_ref, ext_ref = refs
    else:
        u_ref, hist_ref, w_ref, sc_ref, o_ref, ext_ref = refs
        prev_ref = None
    i = pl.program_id(1)
    tm = u_ref.shape[1]

    @pl.when(i == 0)
    def _():
        ext_ref[0:HIST_ROWS, :] = hist_ref[0]

    if has_prev:
        @pl.when(i > 0)
        def _():
            ext_ref[0:HIST_ROWS, :] = prev_ref[0]

    ext_ref[HIST_ROWS:HIST_ROWS + tm, :] = u_ref[0]
    pos = pos0 + i * tm + lax.broadcasted_iota(jnp.int32, (tm, POOL_GROUP), 0)
    for gi, w in enumerate(POOL_WINDOWS):
        cs = slice(gi * POOL_GROUP, (gi + 1) * POOL_GROUP)
        tok = ext_ref[HIST_ROWS:HIST_ROWS + tm, cs]
        tot = tok
        for back in range(1, w):
            tot = tot + ext_ref[HIST_ROWS - back:HIST_ROWS - back + tm, cs]
        cnt = jnp.minimum(w, pos + 1).astype(F32)
        pooled = tot / cnt - tok
        out = jnp.dot(pooled.astype(BF16), w_ref[gi].astype(BF16), preferred_element_type=F32)
        o_ref[0, :, cs] = (out * sc_ref[:, cs]).astype(o_ref.dtype)


def _pool(u, hist, w_pool, scale, batch, t, pos0):
    tm = _row_tile(t, 512)
    nt = t // tm
    has_prev = nt > 1
    u3 = u.reshape(batch, t, D_B)
    in_specs = [pl.BlockSpec((1, tm, D_B), lambda b, i: (b, i, 0))]
    args = [u3]
    if has_prev:
        per = tm // HIST_ROWS
        in_specs.append(pl.BlockSpec((1, HIST_ROWS, D_B), lambda b, i: (b, jnp.maximum(i * per - 1, 0), 0)))
        args.append(u3)
    in_specs += [
        pl.BlockSpec((1, HIST_ROWS, D_B), lambda b, i: (b, 0, 0)),
        pl.BlockSpec((N_POOL, POOL_GROUP, POOL_GROUP), lambda b, i: (0, 0, 0)),
        pl.BlockSpec((1, D_B), lambda b, i: (0, 0)),
    ]
    args += [hist, w_pool, scale.reshape(1, D_B)]
    out = pl.pallas_call(
        functools.partial(_pool_kernel, pos0=pos0, has_prev=has_prev),
        grid=(batch, nt),
        in_specs=in_specs,
        out_specs=pl.BlockSpec((1, tm, D_B), lambda b, i: (b, i, 0)),
        out_shape=jax.ShapeDtypeStruct((batch, t, D_B), BF16),
        scratch_shapes=[pltpu.VMEM((HIST_ROWS + tm, D_B), F32)],
        compiler_params=_params("arbitrary", "arbitrary"),
        name="pool_mix",
    )(*args)
    return out.reshape(batch * t, D_B)


def _gla_kernel(q_ref, k_ref, v_ref, r_ref, ga_ref, w2_ref, bg_ref, gn_ref, s0_ref, o_ref, s_out_ref,
                st_ref, g_ref, *, chunk):
    i = pl.program_id(1)
    ts = q_ref.shape[1]

    @pl.when(i == 0)
    def _():
        st_ref[...] = s0_ref[0]

    z = jnp.dot(ga_ref[0], w2_ref[...], preferred_element_type=F32) + bg_ref[...]
    g_ref[...] = (jnp.minimum(z, 0.0) - jnp.log(1.0 + jnp.exp(-jnp.abs(z)))) / GATE_NORMALIZER

    ri = lax.broadcasted_iota(jnp.int32, (chunk, chunk), 0)
    ci = lax.broadcasted_iota(jnp.int32, (chunk, chunk), 1)
    tril = ci <= ri
    ones_tril = tril.astype(F32)
    qscale = DK_C ** -0.5
    nt = (((1,), (1,)), ((), ()))
    tn = (((0,), (0,)), ((), ()))

    def body(c, carry):
        rows = pl.ds(pl.multiple_of(c * chunk, chunk), chunk)
        b = jnp.dot(ones_tril, g_ref[rows, :], preferred_element_type=F32, precision=lax.Precision.HIGHEST)
        eb = jnp.exp(b)
        enb = jnp.exp(-b)
        bl = jnp.sum(g_ref[rows, :], axis=0, keepdims=True)
        ebl = jnp.exp(bl)
        ekd = jnp.exp(bl - b)
        for h in range(H_C):
            ks = slice(h * DK_P, (h + 1) * DK_P)
            vs = slice(h * DV_P, (h + 1) * DV_P)
            qh = q_ref[0, rows, ks] * qscale
            kh = k_ref[0, rows, ks]
            vh = v_ref[0, rows, vs]
            qe = qh * eb[:, ks]
            ke = kh * enb[:, ks]
            a = jnp.where(tril, lax.dot_general(qe, ke, nt, preferred_element_type=F32), 0.0)
            st = st_ref[h]
            o = jnp.dot(a, vh, preferred_element_type=F32)
            o = o + lax.dot_general(qe, st, nt, preferred_element_type=F32)
            kd = kh * ekd[:, ks]
            st_ref[h] = st * ebl[:, ks] + lax.dot_general(vh, kd, tn, preferred_element_type=F32)
            ms = jnp.sum(o * o, axis=-1, keepdims=True) * (1.0 / DV_C)
            on = o * lax.rsqrt(ms + EPS) * gn_ref[...]
            rh = r_ref[0, rows, vs]
            o_ref[0, rows, vs] = (on * (rh * (1.0 / (1.0 + jnp.exp(-rh))))).astype(o_ref.dtype)
        return carry

    lax.fori_loop(0, ts // chunk, body, 0)

    @pl.when(i == pl.num_programs(1) - 1)
    def _():
        s_out_ref[0] = st_ref[...]


def _gla(proj_c, w2p, bgp, gnp, s0t, batch, t):
    chunk = min(GLA_CHUNK, t)
    out_dtype = BF16 if chunk % (2 * SUBLANE) == 0 else F32
    ts = _row_tile(t, 512)
    assert t % chunk == 0 and ts % chunk == 0
    x = proj_c.reshape(batch, t, C_COLS)
    qk_w, v_w = H_C * DK_P, H_C * DV_P
    col = lambda width, c0: pl.BlockSpec((1, ts, width), lambda b, i: (b, i, c0 // width))
    assert C_K0 % qk_w == 0 and C_V0 % v_w == 0 and C_R0 % v_w == 0 and C_G0 % GATE_P == 0
    full = lambda shape: pl.BlockSpec(shape, lambda b, i: (0,) * len(shape))
    state_spec = pl.BlockSpec((1, H_C, DV_P, DK_P), lambda b, i: (b, 0, 0, 0))
    out, s_new = pl.pallas_call(
        functools.partial(_gla_kernel, chunk=chunk),
        grid=(batch, t // ts),
        in_specs=[col(qk_w, C_Q0), col(qk_w, C_K0), col(v_w, C_V0), col(v_w, C_R0), col(GATE_P, C_G0),
                  full((GATE_P, qk_w)), full((1, qk_w)), full((1, DV_P)), state_spec],
        out_specs=[pl.BlockSpec((1, ts, v_w), lambda b, i: (b, i, 0)), state_spec],
        out_shape=[jax.ShapeDtypeStruct((batch, t, v_w), out_dtype),
                   jax.ShapeDtypeStruct((batch, H_C, DV_P, DK_P), F32)],
        scratch_shapes=[pltpu.VMEM((H_C, DV_P, DK_P), F32), pltpu.VMEM((ts, qk_w), F32)],
        compiler_params=_params("arbitrary", "arbitrary"),
        name="gla",
    )(x, x, x, x, x, w2p, bgp, gnp, s0t)
    return out.reshape(batch * t, v_w).astype(BF16), s_new


def _pad_heads(w, width, padded):
    lead = w.shape[:-1]
    w = w.reshape(lead + (H_C, width))
    w = jnp.pad(w, [(0, 0)] * len(lead) + [(0, 0), (0, padded - width)])
    return w.reshape(lead + (H_C * padded,))


def _relayout_c_weights(w_in, w_gate2, b_gate, gla_norm_g, w_o):
    c0 = 3 * D_A + D_B
    qk = H_C * DK_C
    wq = w_in[:, :, c0:c0 + qk]
    wk = w_in[:, :, c0 + qk:c0 + 2 * qk]
    wv = w_in[:, :, c0 + 2 * qk:c0 + 2 * qk + D_C]
    wr = w_in[:, :, c0 + 2 * qk + D_C:c0 + 2 * qk + 2 * D_C]
    wg = w_in[:, :, c0 + 2 * qk + 2 * D_C:]
    w_c = jnp.concatenate([
        _pad_heads(wq, DK_C, DK_P), _pad_heads(wk, DK_C, DK_P),
        _pad_heads(wv, DV_C, DV_P), _pad_heads(wr, DV_C, DV_P),
        jnp.pad(wg, ((0, 0), (0, 0), (0, GATE_P - GATE_RANK))),
    ], axis=-1)
    w2p = jnp.pad(_pad_heads(w_gate2, DK_C, DK_P), ((0, 0), (0, GATE_P - GATE_RANK), (0, 0)))
    bgp = _pad_heads(b_gate, DK_C, DK_P)[:, None, :]
    gnp = jnp.pad(gla_norm_g, ((0, 0), (0, DV_P - DV_C)))[:, None, :]
    depth, _, d_model = w_o.shape
    wo_c = w_o[:, D_A + D_B:, :].reshape(depth, H_C, DV_C, d_model)
    wo_c = jnp.pad(wo_c, ((0, 0), (0, 0), (0, DV_P - DV_C), (0, 0))).reshape(depth, H_C * DV_P, d_model)
    return w_c, w2p, bgp, gnp, wo_c


def _state_to_kernel(s):
    s = jnp.swapaxes(s, 2, 3)
    return jnp.pad(s, ((0, 0), (0, 0), (0, DV_P - DV_C), (0, DK_P - DK_C)))


def _state_from_kernel(s):
    return jnp.swapaxes(s[:, :, :DV_C, :DK_C], 2, 3)


def _layer(x, batch, t, layer, wts, *, cache=None, pool_hist=None, gla_s0=None, pos0=0):
    (norm1_g, w_in, w_c, w2p, bgp, gnp, w_pool, pool_scale, w_o, wo_c, norm2_g, w_ffn_gate, w_ffn_up,
     w_ffn_down) = wts
    d_model = x.shape[1]
    h = _rmsnorm(x, norm1_g, BF16)
    qkv = _matmul([h], [(w_in, d_model, 0)], 3 * D_A, name="proj_qkv")
    u = _matmul([h], [(w_in, d_model, 3 * D_A)], D_B, name="proj_pool")
    proj_c = _matmul([h], [(w_c, d_model, 0)], C_COLS, name="proj_gla")

    if cache is None:
        parts = [_band_attn(qkv, batch, t, w, d) for (w, d) in DILATED_PATTERNS]
        out_a = _combine(parts)
    else:
        out_a = _dec_attn(qkv, cache[0], cache[1], layer, batch, t)

    out_b = _pool(u, pool_hist, w_pool, pool_scale, batch, t, pos0)
    out_c, s_new = _gla(proj_c, w2p, bgp, gnp, gla_s0, batch, t)

    mix_ab = jnp.concatenate([out_a, out_b], axis=-1)
    x = _matmul([mix_ab, out_c], [(w_o, D_A + D_B, 0), (wo_c, H_C * DV_P, 0)], d_model, res=x,
                name="out_proj")
    h2 = _rmsnorm(x, norm2_g, BF16)
    f = _glu(h2, w_ffn_gate, w_ffn_up)
    x = _matmul([f], [(w_ffn_down, w_ffn_down.shape[0], 0)], d_model, res=x, tm_target=512, name="ffn_down")
    return x, qkv, u, s_new


def kernel(x_prompt, x_sample, cache_win_k, cache_win_v, state_pool, state_gla, norm1_g, w_in, w_gate2, b_gate,
           w_pool, pool_scale, gla_norm_g, w_o, norm2_g, w_ffn_gate, w_ffn_up, w_ffn_down, final_norm_g):
    bp, sp, d_model = x_prompt.shape
    bs, ts, _ = x_sample.shape
    depth = w_in.shape[0]
    past = cache_win_k.shape[2]
    keep = min(max(w for w, _ in DILATED_PATTERNS), sp)

    w_c, w2p, bgp, gnp, wo_c = _relayout_c_weights(w_in, w_gate2, b_gate, gla_norm_g, w_o)
    cache_k = cache_win_k.reshape(depth, bs, past, D_A)
    cache_v = cache_win_v.reshape(depth, bs, past, D_A)

    xp = x_prompt.reshape(bp * sp, d_model)
    xs = x_sample.reshape(bs * ts, d_model)
    zero_hist = jnp.zeros((bp, HIST_ROWS, D_B), F32)
    zero_state = jnp.zeros((bp, H_C, DV_P, DK_P), F32)
    p_k, p_v, p_pool, p_gla, s_k, s_v, s_pool, s_gla = [], [], [], [], [], [], [], []
    for l in range(depth):
        wts = (norm1_g[l], w_in[l], w_c[l], w2p[l], bgp[l], gnp[l], w_pool[l], pool_scale[l], w_o[l], wo_c[l],
               norm2_g[l], w_ffn_gate[l], w_ffn_up[l], w_ffn_down[l])
        xp, qkv, u, s_new = _layer(xp, bp, sp, l, wts, pool_hist=zero_hist, gla_s0=zero_state, pos0=0)
        qkv3 = qkv.reshape(bp, sp, 3 * D_A)
        p_k.append(qkv3[:, sp - keep:, D_A:2 * D_A].reshape(bp, keep, H_A, HD_A))
        p_v.append(qkv3[:, sp - keep:, 2 * D_A:].reshape(bp, keep, H_A, HD_A))
        p_pool.append(u.reshape(bp, sp, D_B)[:, sp - POOL_BUF:])
        p_gla.append(_state_from_kernel(s_new))

        hist = jnp.pad(state_pool[l], ((0, 0), (HIST_ROWS - POOL_BUF, 0), (0, 0)))
        xs, qkv, u, s_new = _layer(xs, bs, ts, l, wts, cache=(cache_k, cache_v), pool_hist=hist,
                                   gla_s0=_state_to_kernel(state_gla[l]), pos0=PAST_LEN)
        qkv3 = qkv.reshape(bs, ts, 3 * D_A)
        s_k.append(qkv3[:, :, D_A:2 * D_A].reshape(bs, ts, H_A, HD_A))
        s_v.append(qkv3[:, :, 2 * D_A:].reshape(bs, ts, H_A, HD_A))
        u_ext = jnp.concatenate([state_pool[l], u.reshape(bs, ts, D_B)], axis=1)
        s_pool.append(u_ext[:, -POOL_BUF:])
        s_gla.append(_state_from_kernel(s_new))

    y_prompt = _rmsnorm(xp, final_norm_g, F32).reshape(bp, sp, d_model)
    y_sample = _rmsnorm(xs, final_norm_g, F32).reshape(bs, ts, d_model)
    return (y_prompt, y_sample, jnp.stack(p_k), jnp.stack(p_v), jnp.stack(p_pool), jnp.stack(p_gla),
            jnp.stack(s_k), jnp.stack(s_v), jnp.stack(s_pool), jnp.stack(s_gla))
```

```python
import functools

import jax
import jax.numpy as jnp
from jax import lax
from jax.experimental import pallas as pl
from jax.experimental.pallas import tpu as pltpu

F32 = jnp.float32
BF16 = jnp.bfloat16

H_A, HD_A = 6, 128
D_A = H_A * HD_A
DILATED_PATTERNS = ((128, 1), (512, 4), (2048, 16))
WIN_MAX = 2048
POOL_WINDOWS = (2, 4, 8, 16)
N_POOL = 4
D_B = 512
POOL_GROUP = D_B // N_POOL
POOL_BUF = 15
H_C, DK_C, DV_C = 4, 96, 192
D_C = H_C * DV_C
GATE_RANK = 16
GATE_NORMALIZER = 16.0
GLA_CHUNK = 32
PAST_LEN = 16384
EPS = 1e-6
NEG_INF = -1e30

LANE = 128
SUBLANE = 8
MXU_N = 256
VMEM_LIMIT_BYTES = 56 * 1024 * 1024

DK_P = 128
DV_P = 256
GATE_P = 256
C_Q0 = 0
C_K0 = C_Q0 + H_C * DK_P
C_V0 = C_K0 + H_C * DK_P
C_R0 = C_V0 + H_C * DV_P
C_G0 = C_R0 + H_C * DV_P
C_COLS = C_G0 + GATE_P
HIST_ROWS = 16

ATTN_BLK = 128
assert all(w // d == ATTN_BLK and d & (d - 1) == 0 for w, d in DILATED_PATTERNS)


def _params(*sem):
    return pltpu.CompilerParams(dimension_semantics=sem, vmem_limit_bytes=VMEM_LIMIT_BYTES)


def _row_tile(m, target):
    if m <= target:
        return m
    t = target
    while t >= SUBLANE:
        if m % t == 0 and t % SUBLANE == 0:
            return t
        t -= SUBLANE
    raise ValueError(f"no row tile for {m}")


def _rmsnorm_kernel(x_ref, g_ref, o_ref):
    x = x_ref[...]
    ms = jnp.mean(x * x, axis=-1, keepdims=True)
    o_ref[...] = (x * lax.rsqrt(ms + EPS) * g_ref[...]).astype(o_ref.dtype)


def _rmsnorm(x, g, out_dtype):
    m, d = x.shape
    tm = _row_tile(m, 512)
    return pl.pallas_call(
        _rmsnorm_kernel,
        grid=(m // tm,),
        in_specs=[pl.BlockSpec((tm, d), lambda i: (i, 0)), pl.BlockSpec((1, d), lambda i: (0, 0))],
        out_specs=pl.BlockSpec((tm, d), lambda i: (i, 0)),
        out_shape=jax.ShapeDtypeStruct((m, d), out_dtype),
        compiler_params=_params("arbitrary"),
        name="rmsnorm",
    )(x, g.reshape(1, d))


def _mm_kernel(*refs, n_in, has_res, slab_out):
    xs = refs[:n_in]
    ws = refs[n_in:2 * n_in]
    pos = 2 * n_in
    res_ref = refs[pos] if has_res else None
    pos += int(has_res)
    o_ref = refs[pos]
    wbs = refs[pos + 1:]

    @pl.when(pl.program_id(1) == 0)
    def _():
        for w_ref, wb_ref in zip(ws, wbs):
            wb_ref[...] = w_ref[...].astype(BF16)

    acc = None
    for x_ref, wb_ref in zip(xs, wbs):
        d = jnp.dot(x_ref[...], wb_ref[...], preferred_element_type=F32)
        acc = d if acc is None else acc + d
    if has_res:
        acc = acc + res_ref[...]
    if slab_out:
        for c in range(o_ref.shape[0]):
            o_ref[c] = acc[:, c * LANE:(c + 1) * LANE].astype(o_ref.dtype)
    else:
        o_ref[...] = acc.astype(o_ref.dtype)


def _matmul(xs, ws, n, *, res=None, out_dtype=F32, tm_target=1024, tn=MXU_N, slab_out=False, name="matmul"):
    m = xs[0].shape[0]
    tm = _row_tile(m, tm_target)
    assert n % tn == 0
    in_specs = []
    for x in xs:
        in_specs.append(pl.BlockSpec((tm, x.shape[1]), lambda j, i: (i, 0)))
    scratch = []
    for (w, layer, k, c0), x in zip(ws, xs):
        assert x.shape[1] == k and c0 % tn == 0
        in_specs.append(pl.BlockSpec((None, k, tn), lambda j, i, c0=c0, layer=layer: (layer, 0, j + c0 // tn)))
        scratch.append(pltpu.VMEM((k, tn), BF16))
    args = list(xs) + [w for w, _, _, _ in ws]
    if res is not None:
        in_specs.append(pl.BlockSpec((tm, tn), lambda j, i: (i, j)))
        args.append(res)
    if slab_out:
        out_spec = pl.BlockSpec((tn // LANE, tm, LANE), lambda j, i: (j, i, 0))
        out_shape = jax.ShapeDtypeStruct((n // LANE, m, LANE), out_dtype)
    else:
        out_spec = pl.BlockSpec((tm, tn), lambda j, i: (i, j))
        out_shape = jax.ShapeDtypeStruct((m, n), out_dtype)
    return pl.pallas_call(
        functools.partial(_mm_kernel, n_in=len(xs), has_res=res is not None, slab_out=slab_out),
        grid=(n // tn, m // tm),
        in_specs=in_specs,
        out_specs=out_spec,
        out_shape=out_shape,
        scratch_shapes=scratch,
        compiler_params=_params("arbitrary", "arbitrary"),
        name=name,
    )(*args)


def _glu_kernel(x_ref, wg_ref, wu_ref, o_ref, wgb_ref, wub_ref):
    @pl.when(pl.program_id(1) == 0)
    def _():
        wgb_ref[...] = wg_ref[...].astype(BF16)
        wub_ref[...] = wu_ref[...].astype(BF16)

    x = x_ref[...]
    g = jnp.dot(x, wgb_ref[...], preferred_element_type=F32)
    u = jnp.dot(x, wub_ref[...], preferred_element_type=F32)
    o_ref[...] = (g * (1.0 / (1.0 + jnp.exp(-g))) * u).astype(o_ref.dtype)


def _glu(x, wg, wu, layer, *, tm_target=1024, tn=MXU_N):
    m, k = x.shape
    n = wg.shape[2]
    tm = _row_tile(m, tm_target)
    assert n % tn == 0
    w_spec = pl.BlockSpec((None, k, tn), lambda j, i: (layer, 0, j))
    return pl.pallas_call(
        _glu_kernel,
        grid=(n // tn, m // tm),
        in_specs=[pl.BlockSpec((tm, k), lambda j, i: (i, 0)), w_spec, w_spec],
        out_specs=pl.BlockSpec((tm, tn), lambda j, i: (i, j)),
        out_shape=jax.ShapeDtypeStruct((m, n), BF16),
        scratch_shapes=[pltpu.VMEM((k, tn), BF16), pltpu.VMEM((k, tn), BF16)],
        compiler_params=_params("arbitrary", "arbitrary"),
        name="ffn_glu",
    )(x, wg, wu)


def _span_attn_kernel(q_ref, k_ref, v_ref, o_ref, pk_ref, pv_ref, k_scr, v_scr, o_scr, l_scr, *, skew):
    n0 = pl.program_id(2)
    span = q_ref.shape[0]
    blk = ATTN_BLK
    units = span // blk
    scale = HD_A ** -0.5
    cur = (n0 % 2) * span
    prev = span - cur

    @pl.when(n0 == 0)
    def _():
        k_scr[pl.ds(prev, span), :] = jnp.zeros((span, HD_A), F32)
        v_scr[pl.ds(prev, span), :] = jnp.zeros((span, HD_A), F32)

    k_all = k_ref[...]
    v_all = v_ref[...]
    k_scr[pl.ds(cur, span), :] = k_all
    v_scr[pl.ds(cur, span), :] = v_all

    @pl.when(n0 == pl.num_programs(2) - 1)
    def _():
        pk_ref[...] = k_all
        pv_ref[...] = v_all

    qi = lax.broadcasted_iota(jnp.int32, (blk, blk), 0)
    ki = lax.broadcasted_iota(jnp.int32, (blk, blk), 1)
    mask_cur = ki <= qi
    nt = (((1,), (1,)), ((), ()))

    def rows(start, dil):
        return pl.ds(start, blk) if dil == 1 else pl.ds(start, blk, stride=dil)

    mask_prev_first = ki >= qi + jnp.where(n0 > 0, 0, blk)
    mask_prev_later = ki >= qi

    def unit_starts(u, dil):
        r, nb = u % dil, u // dil
        start = r + dil * blk * nb
        prev_start = cur + start - dil * blk if nb > 0 else prev + r + span - dil * blk
        return start, prev_start, nb

    def scores(u, dil):
        start, prev_start, nb = unit_starts(u, dil)
        q = q_ref[rows(start, dil), :].astype(BF16)
        kc = k_scr[rows(cur + start, dil), :].astype(BF16)
        kp = k_scr[rows(prev_start, dil), :].astype(BF16)
        sp = lax.dot_general(q, kp, nt, preferred_element_type=F32) * scale
        sc = lax.dot_general(q, kc, nt, preferred_element_type=F32) * scale
        sp = jnp.where(mask_prev_later if nb > 0 else mask_prev_first, sp, NEG_INF)
        sc = jnp.where(mask_cur, sc, NEG_INF)
        return sp, sc

    def finish(p, u, dil, sp, sc):
        start, prev_start, _ = unit_starts(u, dil)
        vc = v_scr[rows(cur + start, dil), :].astype(BF16)
        vp = v_scr[rows(prev_start, dil), :].astype(BF16)
        m = jnp.max(jnp.maximum(sp, sc), axis=-1, keepdims=True)
        pp = jnp.exp(sp - m)
        pc = jnp.exp(sc - m)
        den = jnp.sum(pp + pc, axis=-1, keepdims=True)
        o = jnp.dot(pp.astype(BF16), vp, preferred_element_type=F32)
        o = o + jnp.dot(pc.astype(BF16), vc, preferred_element_type=F32)
        o_scr[p, rows(start, dil), :] = o / den
        l_scr[p, rows(start, dil), :] = jnp.broadcast_to(m + jnp.log(den), (blk, HD_A))

    order = [(p, u, dil) for p, (_, dil) in enumerate(DILATED_PATTERNS) for u in range(units)]
    pending = []
    for step in range(len(order) + skew):
        if step < len(order):
            p, u, dil = order[step]
            pending.append((p, u, dil) + scores(u, dil))
        if step >= skew:
            finish(*pending.pop(0))

    ct = 64

    def combine(c, carry):
        rs = pl.ds(pl.multiple_of(c * ct, ct), ct)
        l1, l2, l3 = l_scr[0, rs, :], l_scr[1, rs, :], l_scr[2, rs, :]
        mx = jnp.maximum(jnp.maximum(l1, l2), l3)
        e1, e2, e3 = jnp.exp(l1 - mx), jnp.exp(l2 - mx), jnp.exp(l3 - mx)
        num = e1 * o_scr[0, rs, :] + e2 * o_scr[1, rs, :] + e3 * o_scr[2, rs, :]
        o_ref[rs, :] = (num / (e1 + e2 + e3)).astype(o_ref.dtype)
        return carry

    lax.fori_loop(0, span // ct, combine, 0, unroll=2)


def _span_attn(qkv_slabs, batch, seq):
    span = WIN_MAX
    assert seq % span == 0 and len(DILATED_PATTERNS) == 3
    nsp = seq // span
    slab = lambda which: pl.BlockSpec((None, span, HD_A), lambda b, h, n: (which * H_A + h, b * nsp + n, 0))
    win_spec = pl.BlockSpec((None, None, span, HD_A), lambda b, h, n: (b, h, 0, 0))
    win_shape = jax.ShapeDtypeStruct((batch, H_A, span, HD_A), F32)
    return pl.pallas_call(
        functools.partial(_span_attn_kernel, skew=4),
        grid=(batch, H_A, nsp),
        in_specs=[slab(0), slab(1), slab(2)],
        out_specs=[pl.BlockSpec((span, HD_A), lambda b, h, n: (b * nsp + n, h)), win_spec, win_spec],
        out_shape=[jax.ShapeDtypeStruct((batch * seq, D_A), BF16), win_shape, win_shape],
        scratch_shapes=[pltpu.VMEM((2 * span, HD_A), F32), pltpu.VMEM((2 * span, HD_A), F32),
                        pltpu.VMEM((3, span, HD_A), F32), pltpu.VMEM((3, span, HD_A), F32)],
        compiler_params=_params("arbitrary", "arbitrary", "arbitrary"),
        name="span_attn",
    )(qkv_slabs, qkv_slabs, qkv_slabs)


def _dec_attn_kernel(q_ref, kn_ref, vn_ref, kc_ref, vc_ref, o_ref, *, past):
    t = q_ref.shape[0]
    scale = HD_A ** -0.5
    nt = (((1,), (1,)), ((), ()))
    q = q_ref[...]
    kc = kc_ref[...]
    vc = vc_ref[...]
    kn = kn_ref[...]
    vn = vn_ref[...]
    s_c = lax.dot_general(q, kc, nt, preferred_element_type=F32) * scale
    s_n = lax.dot_general(q, kn, nt, preferred_element_type=F32) * scale
    d_c = past + lax.broadcasted_iota(jnp.int32, (t, past), 0) - lax.broadcasted_iota(jnp.int32, (t, past), 1)
    d_n = lax.broadcasted_iota(jnp.int32, (t, t), 0) - lax.broadcasted_iota(jnp.int32, (t, t), 1)
    outs, lses = [], []
    for window, dil in DILATED_PATTERNS:
        m_c = jnp.logical_and((d_c & (dil - 1)) == 0, d_c <= window)
        m_n = jnp.logical_and(jnp.logical_and(d_n >= 0, (d_n & (dil - 1)) == 0), d_n <= window)
        a_c = jnp.where(m_c, s_c, NEG_INF)
        a_n = jnp.where(m_n, s_n, NEG_INF)
        m = jnp.maximum(jnp.max(a_c, axis=-1, keepdims=True), jnp.max(a_n, axis=-1, keepdims=True))
        p_c = jnp.exp(a_c - m)
        p_n = jnp.exp(a_n - m)
        den = jnp.sum(p_c, axis=-1, keepdims=True) + jnp.sum(p_n, axis=-1, keepdims=True)
        o = jnp.dot(p_c, vc, preferred_element_type=F32) + jnp.dot(p_n, vn, preferred_element_type=F32)
        outs.append(o / den)
        lses.append(m + jnp.log(den))
    mx = jnp.maximum(jnp.maximum(lses[0], lses[1]), lses[2])
    es = [jnp.exp(l - mx) for l in lses]
    num = es[0] * outs[0] + es[1] * outs[1] + es[2] * outs[2]
    o_ref[...] = (num / (es[0] + es[1] + es[2])).astype(o_ref.dtype)


def _dec_attn(qkv, cache_k, cache_v, layer, batch, t):
    past = cache_k.shape[3]
    new_spec = lambda which: pl.BlockSpec((t, HD_A), lambda b, h: (b, which * H_A + h))
    cache_spec = pl.BlockSpec((None, None, None, past, HD_A), lambda b, h: (layer, b, h, 0, 0))
    return pl.pallas_call(
        functools.partial(_dec_attn_kernel, past=past),
        grid=(batch, H_A),
        in_specs=[new_spec(0), new_spec(1), new_spec(2), cache_spec, cache_spec],
        out_specs=pl.BlockSpec((t, HD_A), lambda b, h: (b, h)),
        out_shape=jax.ShapeDtypeStruct((batch * t, D_A), F32),
        compiler_params=_params("arbitrary", "arbitrary"),
        name="dec_attn",
    )(qkv, qkv, qkv, cache_k, cache_v).astype(BF16)


def _pool_kernel(*refs, pos0, has_prev):
    if has_prev:
        u_ref, prev_ref, hist_ref, w_ref, sc_ref, o_ref, ext_ref = refs
    else:
        u_ref, hist_ref, w_ref, sc_ref, o_ref, ext_ref = refs
        prev_ref = None
    i = pl.program_id(1)
    tm = u_ref.shape[1]

    @pl.when(i == 0)
    def _():
        ext_ref[0:HIST_ROWS, :] = hist_ref[0]

    if has_prev:
        @pl.when(i > 0)
        def _():
            ext_ref[0:HIST_ROWS, :] = prev_ref[0]

    ext_ref[HIST_ROWS:HIST_ROWS + tm, :] = u_ref[0]
    pos = pos0 + i * tm + lax.broadcasted_iota(jnp.int32, (tm, POOL_GROUP), 0)
    for gi, w in enumerate(POOL_WINDOWS):
        cs = slice(gi * POOL_GROUP, (gi + 1) * POOL_GROUP)
        tok = ext_ref[HIST_ROWS:HIST_ROWS + tm, cs]
        tot = tok
        for back in range(1, w):
            tot = tot + ext_ref[HIST_ROWS - back:HIST_ROWS - back + tm, cs]
        cnt = jnp.minimum(w, pos + 1).astype(F32)
        pooled = tot / cnt - tok
        out = jnp.dot(pooled.astype(BF16), w_ref[gi].astype(BF16), preferred_element_type=F32)
        o_ref[0, :, cs] = (out * sc_ref[:, cs]).astype(o_ref.dtype)


def _pool(u, hist, w_pool, scale, layer, batch, t, pos0):
    tm = _row_tile(t, 512)
    nt = t // tm
    has_prev = nt > 1
    u3 = u.reshape(batch, t, D_B)
    in_specs = [pl.BlockSpec((1, tm, D_B), lambda b, i: (b, i, 0))]
    args = [u3]
    if has_prev:
        per = tm // HIST_ROWS
        in_specs.append(pl.BlockSpec((1, HIST_ROWS, D_B), lambda b, i: (b, jnp.maximum(i * per - 1, 0), 0)))
        args.append(u3)
    in_specs += [
        pl.BlockSpec((1, HIST_ROWS, D_B), lambda b, i: (b, 0, 0)),
        pl.BlockSpec((None, N_POOL, POOL_GROUP, POOL_GROUP), lambda b, i: (layer, 0, 0, 0)),
        pl.BlockSpec((None, 1, D_B), lambda b, i: (layer, 0, 0)),
    ]
    args += [hist, w_pool, scale]
    out = pl.pallas_call(
        functools.partial(_pool_kernel, pos0=pos0, has_prev=has_prev),
        grid=(batch, nt),
        in_specs=in_specs,
        out_specs=pl.BlockSpec((1, tm, D_B), lambda b, i: (b, i, 0)),
        out_shape=jax.ShapeDtypeStruct((batch, t, D_B), BF16),
        scratch_shapes=[pltpu.VMEM((HIST_ROWS + tm, D_B), F32)],
        compiler_params=_params("arbitrary", "arbitrary"),
        name="pool_mix",
    )(*args)
    return out.reshape(batch * t, D_B)


def _gla_kernel(q_ref, k_ref, v_ref, r_ref, ga_ref, w2_ref, bg_ref, gn_ref, s0_ref, o_ref, s_out_ref,
                st_ref, g_ref, *, chunk):
    i = pl.program_id(1)
    ts = q_ref.shape[1]

    @pl.when(i == 0)
    def _():
        st_ref[...] = s0_ref[0]

    z = jnp.dot(ga_ref[0], w2_ref[...], preferred_element_type=F32) + bg_ref[...]
    g_ref[...] = (jnp.minimum(z, 0.0) - jnp.log(1.0 + jnp.exp(-jnp.abs(z)))) / GATE_NORMALIZER

    ri = lax.broadcasted_iota(jnp.int32, (chunk, chunk), 0)
    ci = lax.broadcasted_iota(jnp.int32, (chunk, chunk), 1)
    tril = ci <= ri
    ones_tril = tril.astype(F32)
    qscale = DK_C ** -0.5
    nt = (((1,), (1,)), ((), ()))
    tn = (((0,), (0,)), ((), ()))

    def body(c, carry):
        rows = pl.ds(pl.multiple_of(c * chunk, chunk), chunk)
        b = jnp.dot(ones_tril, g_ref[rows, :], preferred_element_type=F32, precision=lax.Precision.HIGHEST)
        eb = jnp.exp(b)
        enb = jnp.exp(-b)
        bl = jnp.sum(g_ref[rows, :], axis=0, keepdims=True)
        ebl = jnp.exp(bl)
        ekd = jnp.exp(bl - b)
        for h in range(H_C):
            ks = slice(h * DK_P, (h + 1) * DK_P)
            vs = slice(h * DV_P, (h + 1) * DV_P)
            qh = q_ref[0, rows, ks] * qscale
            kh = k_ref[0, rows, ks]
            vh = v_ref[0, rows, vs]
            qe = qh * eb[:, ks]
            ke = kh * enb[:, ks]
            a = jnp.where(tril, lax.dot_general(qe, ke, nt, preferred_element_type=F32), 0.0)
            st = st_ref[h]
            o = jnp.dot(a, vh, preferred_element_type=F32)
            o = o + lax.dot_general(qe, st, nt, preferred_element_type=F32)
            kd = kh * ekd[:, ks]
            st_ref[h] = st * ebl[:, ks] + lax.dot_general(vh, kd, tn, preferred_element_type=F32)
            ms = jnp.sum(o * o, axis=-1, keepdims=True) * (1.0 / DV_C)
            on = o * lax.rsqrt(ms + EPS) * gn_ref[...]
            rh = r_ref[0, rows, vs]
            o_ref[0, rows, vs] = (on * (rh * (1.0 / (1.0 + jnp.exp(-rh))))).astype(o_ref.dtype)
        return carry

    lax.fori_loop(0, ts // chunk, body, 0)

    @pl.when(i == pl.num_programs(1) - 1)
    def _():
        s_out_ref[0] = st_ref[...]


def _gla(proj_c, w2p, bgp, gnp, s0t, layer, batch, t):
    chunk = min(GLA_CHUNK, t)
    out_dtype = BF16 if chunk % (2 * SUBLANE) == 0 else F32
    ts = _row_tile(t, 512)
    assert t % chunk == 0 and ts % chunk == 0
    x = proj_c.reshape(batch, t, C_COLS)
    qk_w, v_w = H_C * DK_P, H_C * DV_P
    col = lambda width, c0: pl.BlockSpec((1, ts, width), lambda b, i: (b, i, c0 // width))
    assert C_K0 % qk_w == 0 and C_V0 % v_w == 0 and C_R0 % v_w == 0 and C_G0 % GATE_P == 0
    par = lambda shape: pl.BlockSpec((None,) + shape, lambda b, i: (layer,) + (0,) * len(shape))
    state_spec = pl.BlockSpec((1, H_C, DV_P, DK_P), lambda b, i: (b, 0, 0, 0))
    out, s_new = pl.pallas_call(
        functools.partial(_gla_kernel, chunk=chunk),
        grid=(batch, t // ts),
        in_specs=[col(qk_w, C_Q0), col(qk_w, C_K0), col(v_w, C_V0), col(v_w, C_R0), col(GATE_P, C_G0),
                  par((GATE_P, qk_w)), par((1, qk_w)), par((1, DV_P)), state_spec],
        out_specs=[pl.BlockSpec((1, ts, v_w), lambda b, i: (b, i, 0)), state_spec],
        out_shape=[jax.ShapeDtypeStruct((batch, t, v_w), out_dtype),
                   jax.ShapeDtypeStruct((batch, H_C, DV_P, DK_P), F32)],
        scratch_shapes=[pltpu.VMEM((H_C, DV_P, DK_P), F32), pltpu.VMEM((ts, qk_w), F32)],
        compiler_params=_params("arbitrary", "arbitrary"),
        name="gla",
    )(x, x, x, x, x, w2p, bgp, gnp, s0t)
    return out.reshape(batch * t, v_w).astype(BF16), s_new


def _pad_heads(w, width, padded):
    lead = w.shape[:-1]
    w = w.reshape(lead + (H_C, width))
    w = jnp.pad(w, [(0, 0)] * len(lead) + [(0, 0), (0, padded - width)])
    return w.reshape(lead + (H_C * padded,))


def _relayout_c_weights(w_in, w_gate2, b_gate, gla_norm_g, w_o):
    c0 = 3 * D_A + D_B
    qk = H_C * DK_C
    wq = w_in[:, :, c0:c0 + qk]
    wk = w_in[:, :, c0 + qk:c0 + 2 * qk]
    wv = w_in[:, :, c0 + 2 * qk:c0 + 2 * qk + D_C]
    wr = w_in[:, :, c0 + 2 * qk + D_C:c0 + 2 * qk + 2 * D_C]
    wg = w_in[:, :, c0 + 2 * qk + 2 * D_C:]
    w_c = jnp.concatenate([
        _pad_heads(wq, DK_C, DK_P), _pad_heads(wk, DK_C, DK_P),
        _pad_heads(wv, DV_C, DV_P), _pad_heads(wr, DV_C, DV_P),
        jnp.pad(wg, ((0, 0), (0, 0), (0, GATE_P - GATE_RANK))),
    ], axis=-1)
    w2p = jnp.pad(_pad_heads(w_gate2, DK_C, DK_P), ((0, 0), (0, GATE_P - GATE_RANK), (0, 0)))
    bgp = _pad_heads(b_gate, DK_C, DK_P)[:, None, :]
    gnp = jnp.pad(gla_norm_g, ((0, 0), (0, DV_P - DV_C)))[:, None, :]
    depth, _, d_model = w_o.shape
    wo_c = w_o[:, D_A + D_B:, :].reshape(depth, H_C, DV_C, d_model)
    wo_c = jnp.pad(wo_c, ((0, 0), (0, 0), (0, DV_P - DV_C), (0, 0))).reshape(depth, H_C * DV_P, d_model)
    return w_c, w2p, bgp, gnp, wo_c


def _state_to_kernel(s):
    s = jnp.swapaxes(s, 2, 3)
    return jnp.pad(s, ((0, 0), (0, 0), (0, DV_P - DV_C), (0, DK_P - DK_C)))


def _state_from_kernel(s):
    return jnp.swapaxes(s[:, :, :DV_C, :DK_C], 2, 3)


def _layer(x, batch, t, layer, wts, *, cache=None, pool_hist=None, gla_s0=None, pos0=0):
    (norm1_g, w_in, w_c, w2p, bgp, gnp, w_pool, pool_scale, w_o, wo_c, norm2_g, w_ffn_gate, w_ffn_up,
     w_ffn_down) = wts
    d_model = x.shape[1]
    h = _rmsnorm(x, norm1_g[layer], BF16)
    u = _matmul([h], [(w_in, layer, d_model, 3 * D_A)], D_B, name="proj_pool")
    proj_c = _matmul([h], [(w_c, layer, d_model, 0)], C_COLS, name="proj_gla")

    if cache is None:
        qkv = _matmul([h], [(w_in, layer, d_model, 0)], 3 * D_A, slab_out=True, name="proj_qkv")
        out_a, ka, va = _span_attn(qkv, batch, t)
    else:
        qkv = _matmul([h], [(w_in, layer, d_model, 0)], 3 * D_A, name="proj_qkv")
        out_a = _dec_attn(qkv, cache[0], cache[1], layer, batch, t)
        qkv3 = qkv.reshape(batch, t, 3, H_A, HD_A)
        ka = jnp.swapaxes(qkv3[:, :, 1], 1, 2)
        va = jnp.swapaxes(qkv3[:, :, 2], 1, 2)

    out_b = _pool(u, pool_hist, w_pool, pool_scale, layer, batch, t, pos0)
    out_c, s_new = _gla(proj_c, w2p, bgp, gnp, gla_s0, layer, batch, t)

    mix_ab = jnp.concatenate([out_a, out_b], axis=-1)
    x = _matmul([mix_ab, out_c], [(w_o, layer, D_A + D_B, 0), (wo_c, layer, H_C * DV_P, 0)], d_model, res=x,
                name="out_proj")
    h2 = _rmsnorm(x, norm2_g[layer], BF16)
    f = _glu(h2, w_ffn_gate, w_ffn_up, layer)
    x = _matmul([f], [(w_ffn_down, layer, w_ffn_down.shape[1], 0)], d_model, res=x, tm_target=512,
                name="ffn_down")
    return x, ka, va, u, s_new


def _heads_to_rows(a):
    return jnp.transpose(a, (0, 1, 3, 2, 4))


def kernel(x_prompt, x_sample, cache_win_k, cache_win_v, state_pool, state_gla, norm1_g, w_in, w_gate2, b_gate,
           w_pool, pool_scale, gla_norm_g, w_o, norm2_g, w_ffn_gate, w_ffn_up, w_ffn_down, final_norm_g):
    bp, sp, d_model = x_prompt.shape
    bs, ts, _ = x_sample.shape
    depth = w_in.shape[0]
    assert min(WIN_MAX, sp) == WIN_MAX

    w_c, w2p, bgp, gnp, wo_c = _relayout_c_weights(w_in, w_gate2, b_gate, gla_norm_g, w_o)
    cache_k = jnp.transpose(cache_win_k, (0, 1, 3, 2, 4))
    cache_v = jnp.transpose(cache_win_v, (0, 1, 3, 2, 4))
    wts = (norm1_g, w_in, w_c, w2p, bgp, gnp, w_pool, pool_scale[:, None, :], w_o, wo_c, norm2_g, w_ffn_gate,
           w_ffn_up, w_ffn_down)

    xp = x_prompt.reshape(bp * sp, d_model)
    xs = x_sample.reshape(bs * ts, d_model)
    zero_hist = jnp.zeros((bp, HIST_ROWS, D_B), F32)
    zero_state = jnp.zeros((bp, H_C, DV_P, DK_P), F32)
    p_k, p_v, p_pool, p_gla, s_k, s_v, s_pool, s_gla = [], [], [], [], [], [], [], []
    for l in range(depth):
        xp, ka, va, u, s_new = _layer(xp, bp, sp, l, wts, pool_hist=zero_hist, gla_s0=zero_state, pos0=0)
        p_k.append(ka)
        p_v.append(va)
        p_pool.append(u.reshape(bp, sp, D_B)[:, sp - POOL_BUF:])
        p_gla.append(_state_from_kernel(s_new))

        hist = jnp.pad(state_pool[l], ((0, 0), (HIST_ROWS - POOL_BUF, 0), (0, 0)))
        xs, ka, va, u, s_new = _layer(xs, bs, ts, l, wts, cache=(cache_k, cache_v), pool_hist=hist,
                                      gla_s0=_state_to_kernel(state_gla[l]), pos0=PAST_LEN)
        s_k.append(ka)
        s_v.append(va)
        u_ext = jnp.concatenate([state_pool[l], u.reshape(bs, ts, D_B)], axis=1)
        s_pool.append(u_ext[:, -POOL_BUF:])
        s_gla.append(_state_from_kernel(s_new))

    y_prompt = _rmsnorm(xp, final_norm_g, F32).reshape(bp, sp, d_model)
    y_sample = _rmsnorm(xs, final_norm_g, F32).reshape(bs, ts, d_model)
    return (y_prompt, y_sample, _heads_to_rows(jnp.stack(p_k)), _heads_to_rows(jnp.stack(p_v)),
            jnp.stack(p_pool), jnp.stack(p_gla), _heads_to_rows(jnp.stack(s_k)), _heads_to_rows(jnp.stack(s_v)),
            jnp.stack(s_pool), jnp.stack(s_gla))
```

```python
import functools

import jax
import jax.numpy as jnp
from jax import lax
from jax.experimental import pallas as pl
from jax.experimental.pallas import tpu as pltpu

F32 = jnp.float32
BF16 = jnp.bfloat16

H_A, HD_A = 6, 128
D_A = H_A * HD_A
DILATED_PATTERNS = ((128, 1), (512, 4), (2048, 16))
WIN_MAX = 2048
POOL_WINDOWS = (2, 4, 8, 16)
N_POOL = 4
D_B = 512
POOL_GROUP = D_B // N_POOL
POOL_BUF = 15
H_C, DK_C, DV_C = 4, 96, 192
D_C = H_C * DV_C
GATE_RANK = 16
GATE_NORMALIZER = 16.0
GLA_CHUNK = 32
PAST_LEN = 16384
EPS = 1e-6
NEG_INF = -1e30

LANE = 128
SUBLANE = 8
PACK = 16
VMEM_LIMIT_BYTES = 56 * 1024 * 1024

DK_P = 128
DV_P = 256
HIST_ROWS = 16

VS = DV_P // LANE
SLAB_VC = 0
SLAB_RC = SLAB_VC + H_C * VS
SLAB_QC = SLAB_RC + H_C * VS
SLAB_KC = SLAB_QC + H_C
SLAB_U = SLAB_KC + H_C
SLAB_G = SLAB_U + N_POOL
SLAB_Q = SLAB_G + 2
SLAB_K = SLAB_Q + H_A
SLAB_V = SLAB_K + H_A
N_SLABS = SLAB_V + H_A
D_MIX_P = D_A + D_B + H_C * DV_P

ATTN_BLK = 128
assert all(w // d == ATTN_BLK and d & (d - 1) == 0 for w, d in DILATED_PATTERNS)

TM_ROWS = 1376
TM_NORM = 688
TN_PROJ = 768
TN_WIDE = 512


def _params(*sem):
    return pltpu.CompilerParams(dimension_semantics=sem, vmem_limit_bytes=VMEM_LIMIT_BYTES)


def _row_tile(m, target, mult=PACK):
    if m <= target:
        return m
    t = target - target % mult
    while t >= mult:
        if m % t == 0:
            return t
        t -= mult
    raise ValueError(f"no row tile for {m}")


def _rmsnorm_kernel(x_ref, g_ref, o_ref):
    x = x_ref[...]
    ms = jnp.mean(x * x, axis=-1, keepdims=True)
    o_ref[...] = (x * lax.rsqrt(ms + EPS) * g_ref[...]).astype(o_ref.dtype)


def _rmsnorm(x, g, out_dtype, *, row0=0, rows=None):
    d = x.shape[1]
    rows = x.shape[0] if rows is None else rows
    tm = _row_tile(rows, TM_NORM)
    assert row0 % tm == 0
    return pl.pallas_call(
        _rmsnorm_kernel,
        grid=(rows // tm,),
        in_specs=[pl.BlockSpec((tm, d), lambda i: (i + row0 // tm, 0)), pl.BlockSpec((1, d), lambda i: (0, 0))],
        out_specs=pl.BlockSpec((tm, d), lambda i: (i, 0)),
        out_shape=jax.ShapeDtypeStruct((rows, d), out_dtype),
        compiler_params=_params("arbitrary"),
        name="rmsnorm",
    )(x, g.reshape(1, d))


def _proj_kernel(x_ref, w_ref, o_ref):
    acc = lax.dot_general(x_ref[...], w_ref[...], (((1,), (1,)), ((), ())), preferred_element_type=F32)
    for c in range(o_ref.shape[0]):
        o_ref[c] = acc[:, c * LANE:(c + 1) * LANE]


def _proj(x, w_all, layer):
    m, k = x.shape
    n = w_all.shape[1]
    tm = _row_tile(m, TM_ROWS)
    tn = TN_PROJ
    assert n % tn == 0
    return pl.pallas_call(
        _proj_kernel,
        grid=(n // tn, m // tm),
        in_specs=[pl.BlockSpec((tm, k), lambda j, i: (i, 0)),
                  pl.BlockSpec((None, tn, k), lambda j, i: (layer, j, 0))],
        out_specs=pl.BlockSpec((tn // LANE, tm, LANE), lambda j, i: (j, i, 0)),
        out_shape=jax.ShapeDtypeStruct((n // LANE, m, LANE), F32),
        compiler_params=_params("arbitrary", "arbitrary"),
        name="proj_in",
    )(x, w_all)


def _mm_res_kernel(*refs, n_in):
    xs = refs[:n_in]
    w_ref, res_ref, o_ref, wb_ref = refs[n_in:]

    @pl.when(pl.program_id(1) == 0)
    def _():
        wb_ref[...] = w_ref[...].astype(BF16)

    acc = res_ref[...]
    k0 = 0
    for x_ref in xs:
        k = x_ref.shape[1]
        acc = acc + jnp.dot(x_ref[...], wb_ref[k0:k0 + k, :], preferred_element_type=F32)
        k0 += k
    o_ref[...] = acc


def _matmul_res(xs, x_cols, w, layer, k0, res, name):
    m, n = res.shape
    tm = _row_tile(m, TM_ROWS)
    tn = TN_WIDE
    ktot = sum(width for width, _ in x_cols)
    assert n % tn == 0 and k0 % ktot == 0
    in_specs = [pl.BlockSpec((tm, width), lambda j, i, cb=cb: (i, cb)) for width, cb in x_cols]
    in_specs += [pl.BlockSpec((None, ktot, tn), lambda j, i: (layer, k0 // ktot, j)),
                 pl.BlockSpec((tm, tn), lambda j, i: (i, j))]
    return pl.pallas_call(
        functools.partial(_mm_res_kernel, n_in=len(xs)),
        grid=(n // tn, m // tm),
        in_specs=in_specs,
        out_specs=pl.BlockSpec((tm, tn), lambda j, i: (i, j)),
        out_shape=jax.ShapeDtypeStruct((m, n), F32),
        scratch_shapes=[pltpu.VMEM((ktot, tn), BF16)],
        input_output_aliases={len(xs) + 1: 0},
        compiler_params=_params("arbitrary", "arbitrary"),
        name=name,
    )(*xs, w, res)


def _glu_kernel(x_ref, wg_ref, wu_ref, o_ref, wgb_ref, wub_ref, *, row_split):
    @pl.when(pl.program_id(1) == 0)
    def _():
        wgb_ref[...] = wg_ref[...].astype(BF16)
        wub_ref[...] = wu_ref[...].astype(BF16)

    rows = x_ref.shape[0] // row_split
    for r in range(row_split):
        rs = slice(r * rows, (r + 1) * rows)
        x = x_ref[rs, :]
        g = jnp.dot(x, wgb_ref[...], preferred_element_type=F32)
        u = jnp.dot(x, wub_ref[...], preferred_element_type=F32)
        o_ref[rs, :] = (g * (1.0 / (1.0 + jnp.exp(-g))) * u).astype(o_ref.dtype)


def _glu(x, wg, wu, layer):
    m, k = x.shape
    n = wg.shape[2]
    tm = _row_tile(m, TM_ROWS)
    tn = TN_WIDE
    row_split = 2 if tm % (2 * PACK) == 0 else 1
    assert n % tn == 0
    w_spec = pl.BlockSpec((None, k, tn), lambda j, i: (layer, 0, j))
    return pl.pallas_call(
        functools.partial(_glu_kernel, row_split=row_split),
        grid=(n // tn, m // tm),
        in_specs=[pl.BlockSpec((tm, k), lambda j, i: (i, 0)), w_spec, w_spec],
        out_specs=pl.BlockSpec((tm, tn), lambda j, i: (i, j)),
        out_shape=jax.ShapeDtypeStruct((m, n), BF16),
        scratch_shapes=[pltpu.VMEM((k, tn), BF16), pltpu.VMEM((k, tn), BF16)],
        compiler_params=_params("arbitrary", "arbitrary"),
        name="ffn_glu",
    )(x, wg, wu)


def _span_attn_kernel(q_ref, k_ref, v_ref, o_ref, pk_ref, pv_ref, k_scr, v_scr, o_scr, l_scr, *, skew):
    n0 = pl.program_id(2)
    span = q_ref.shape[0]
    blk = ATTN_BLK
    units = span // blk
    scale = HD_A ** -0.5
    cur = (n0 % 2) * span
    prev = span - cur

    @pl.when(n0 == 0)
    def _():
        k_scr[pl.ds(prev, span), :] = jnp.zeros((span, HD_A), F32)
        v_scr[pl.ds(prev, span), :] = jnp.zeros((span, HD_A), F32)

    k_all = k_ref[...]
    v_all = v_ref[...]
    k_scr[pl.ds(cur, span), :] = k_all
    v_scr[pl.ds(cur, span), :] = v_all

    @pl.when(n0 == pl.num_programs(2) - 1)
    def _():
        pk_ref[...] = k_all
        pv_ref[...] = v_all

    qi = lax.broadcasted_iota(jnp.int32, (blk, blk), 0)
    ki = lax.broadcasted_iota(jnp.int32, (blk, blk), 1)
    mask_cur = ki <= qi
    nt = (((1,), (1,)), ((), ()))

    def rows(start, dil):
        return pl.ds(start, blk) if dil == 1 else pl.ds(start, blk, stride=dil)

    mask_prev_first = ki >= qi + jnp.where(n0 > 0, 0, blk)
    mask_prev_later = ki >= qi

    def unit_starts(u, dil):
        r, nb = u % dil, u // dil
        start = r + dil * blk * nb
        prev_start = cur + start - dil * blk if nb > 0 else prev + r + span - dil * blk
        return start, prev_start, nb

    def scores(u, dil):
        start, prev_start, nb = unit_starts(u, dil)
        q = q_ref[rows(start, dil), :].astype(BF16)
        kc = k_scr[rows(cur + start, dil), :].astype(BF16)
        kp = k_scr[rows(prev_start, dil), :].astype(BF16)
        sp = lax.dot_general(q, kp, nt, preferred_element_type=F32) * scale
        sc = lax.dot_general(q, kc, nt, preferred_element_type=F32) * scale
        sp = jnp.where(mask_prev_later if nb > 0 else mask_prev_first, sp, NEG_INF)
        sc = jnp.where(mask_cur, sc, NEG_INF)
        return sp, sc

    def finish(p, u, dil, sp, sc):
        start, prev_start, _ = unit_starts(u, dil)
        vc = v_scr[rows(cur + start, dil), :].astype(BF16)
        vp = v_scr[rows(prev_start, dil), :].astype(BF16)
        m = jnp.max(jnp.maximum(sp, sc), axis=-1, keepdims=True)
        pp = jnp.exp(sp - m)
        pc = jnp.exp(sc - m)
        den = jnp.sum(pp + pc, axis=-1, keepdims=True)
        o = jnp.dot(pp.astype(BF16), vp, preferred_element_type=F32)
        o = o + jnp.dot(pc.astype(BF16), vc, preferred_element_type=F32)
        o_scr[p, rows(start, dil), :] = o / den
        l_scr[p, rows(start, dil), :] = jnp.broadcast_to(m + jnp.log(den), (blk, HD_A))

    order = [(p, u, dil) for p, (_, dil) in enumerate(DILATED_PATTERNS) for u in range(units)]
    pending = []
    for step in range(len(order) + skew):
        if step < len(order):
            p, u, dil = order[step]
            pending.append((p, u, dil) + scores(u, dil))
        if step >= skew:
            finish(*pending.pop(0))

    ct = 64

    def combine(c, carry):
        rs = pl.ds(pl.multiple_of(c * ct, ct), ct)
        l1, l2, l3 = l_scr[0, rs, :], l_scr[1, rs, :], l_scr[2, rs, :]
        mx = jnp.maximum(jnp.maximum(l1, l2), l3)
        e1, e2, e3 = jnp.exp(l1 - mx), jnp.exp(l2 - mx), jnp.exp(l3 - mx)
        num = e1 * o_scr[0, rs, :] + e2 * o_scr[1, rs, :] + e3 * o_scr[2, rs, :]
        o_ref[rs, :] = (num / (e1 + e2 + e3)).astype(o_ref.dtype)
        return carry

    lax.fori_loop(0, span // ct, combine, 0, unroll=2)


def _span_attn(slabs, batch, seq):
    span = WIN_MAX
    assert seq % span == 0 and len(DILATED_PATTERNS) == 3
    nsp = seq // span
    slab = lambda s0: pl.BlockSpec((None, span, HD_A), lambda b, h, n: (s0 + h, b * nsp + n, 0))
    win_spec = pl.BlockSpec((None, None, span, HD_A), lambda b, h, n: (b, h, 0, 0))
    win_shape = jax.ShapeDtypeStruct((batch, H_A, span, HD_A), F32)
    return pl.pallas_call(
        functools.partial(_span_attn_kernel, skew=4),
        grid=(batch, H_A, nsp),
        in_specs=[slab(SLAB_Q), slab(SLAB_K), slab(SLAB_V)],
        out_specs=[pl.BlockSpec((span, HD_A), lambda b, h, n: (b * nsp + n, h)), win_spec, win_spec],
        out_shape=[jax.ShapeDtypeStruct((slabs.shape[1], D_A), BF16), win_shape, win_shape],
        scratch_shapes=[pltpu.VMEM((2 * span, HD_A), F32), pltpu.VMEM((2 * span, HD_A), F32),
                        pltpu.VMEM((3, span, HD_A), F32), pltpu.VMEM((3, span, HD_A), F32)],
        compiler_params=_params("arbitrary", "arbitrary", "arbitrary"),
        name="span_attn",
    )(slabs, slabs, slabs)


def _dec_attn_kernel(q_ref, kn_ref, vn_ref, kc_ref, vc_ref, o_ref, *, past):
    t = q_ref.shape[0]
    scale = HD_A ** -0.5
    nt = (((1,), (1,)), ((), ()))
    q = q_ref[...]
    kc = kc_ref[...]
    vc = vc_ref[...]
    kn = kn_ref[...]
    vn = vn_ref[...]
    s_c = lax.dot_general(q, kc, nt, preferred_element_type=F32) * scale
    s_n = lax.dot_general(q, kn, nt, preferred_element_type=F32) * scale
    d_c = past + lax.broadcasted_iota(jnp.int32, (t, past), 0) - lax.broadcasted_iota(jnp.int32, (t, past), 1)
    d_n = lax.broadcasted_iota(jnp.int32, (t, t), 0) - lax.broadcasted_iota(jnp.int32, (t, t), 1)
    outs, lses = [], []
    for window, dil in DILATED_PATTERNS:
        m_c = jnp.logical_and((d_c & (dil - 1)) == 0, d_c <= window)
        m_n = jnp.logical_and(jnp.logical_and(d_n >= 0, (d_n & (dil - 1)) == 0), d_n <= window)
        a_c = jnp.where(m_c, s_c, NEG_INF)
        a_n = jnp.where(m_n, s_n, NEG_INF)
        m = jnp.maximum(jnp.max(a_c, axis=-1, keepdims=True), jnp.max(a_n, axis=-1, keepdims=True))
        p_c = jnp.exp(a_c - m)
        p_n = jnp.exp(a_n - m)
        den = jnp.sum(p_c, axis=-1, keepdims=True) + jnp.sum(p_n, axis=-1, keepdims=True)
        o = jnp.dot(p_c, vc, preferred_element_type=F32) + jnp.dot(p_n, vn, preferred_element_type=F32)
        outs.append(o / den)
        lses.append(m + jnp.log(den))
    mx = jnp.maximum(jnp.maximum(lses[0], lses[1]), lses[2])
    es = [jnp.exp(l - mx) for l in lses]
    num = es[0] * outs[0] + es[1] * outs[1] + es[2] * outs[2]
    o_ref[...] = (num / (es[0] + es[1] + es[2])).astype(o_ref.dtype)


def _dec_attn(slabs, row0, cache_k, cache_v, layer, batch, t):
    past = cache_k.shape[3]
    assert row0 % t == 0
    new_spec = lambda s0: pl.BlockSpec((None, t, HD_A), lambda b, h: (s0 + h, row0 // t + b, 0))
    cache_spec = pl.BlockSpec((None, None, None, past, HD_A), lambda b, h: (layer, b, h, 0, 0))
    return pl.pallas_call(
        functools.partial(_dec_attn_kernel, past=past),
        grid=(batch, H_A),
        in_specs=[new_spec(SLAB_Q), new_spec(SLAB_K), new_spec(SLAB_V), cache_spec, cache_spec],
        out_specs=pl.BlockSpec((t, HD_A), lambda b, h: (b, h)),
        out_shape=jax.ShapeDtypeStruct((batch * t, D_A), F32),
        compiler_params=_params("arbitrary", "arbitrary"),
        name="dec_attn",
    )(slabs, slabs, slabs, cache_k, cache_v)


def _pool_kernel(*refs, pos0, has_prev):
    if has_prev:
        u_ref, prev_ref, hist_ref, w_ref, sc_ref, o_ref, ext_ref = refs
    else:
        u_ref, hist_ref, w_ref, sc_ref, o_ref, ext_ref = refs
        prev_ref = None
    i = pl.program_id(1)
    tm = u_ref.shape[1]

    @pl.when(i == 0)
    def _():
        for gi in range(N_POOL):
            ext_ref[gi, 0:HIST_ROWS, :] = hist_ref[0, :, gi * POOL_GROUP:(gi + 1) * POOL_GROUP]

    if has_prev:
        @pl.when(i > 0)
        def _():
            ext_ref[:, 0:HIST_ROWS, :] = prev_ref[...]

    ext_ref[:, HIST_ROWS:HIST_ROWS + tm, :] = u_ref[...]
    pos = pos0 + i * tm + lax.broadcasted_iota(jnp.int32, (tm, POOL_GROUP), 0)
    for gi, w in enumerate(POOL_WINDOWS):
        cs = slice(gi * POOL_GROUP, (gi + 1) * POOL_GROUP)
        tok = ext_ref[gi, HIST_ROWS:HIST_ROWS + tm, :]
        tot = tok
        for back in range(1, w):
            tot = tot + ext_ref[gi, HIST_ROWS - back:HIST_ROWS - back + tm, :]
        cnt = jnp.minimum(w, pos + 1).astype(F32)
        pooled = tot / cnt - tok
        out = jnp.dot(pooled.astype(BF16), w_ref[gi].astype(BF16), preferred_element_type=F32)
        o_ref[:, cs] = (out * sc_ref[:, cs]).astype(o_ref.dtype)


def _pool(slabs, row0, out_rows, hist, w_pool, scale, layer, batch, t, pos0):
    tm = _row_tile(t, 512)
    nt = t // tm
    has_prev = nt > 1
    assert row0 % tm == 0
    rb = lambda b, i: row0 // tm + b * nt + i
    ob = lambda b, i: b * nt + i
    in_specs = [pl.BlockSpec((N_POOL, tm, LANE), lambda b, i: (SLAB_U // N_POOL, rb(b, i), 0))]
    args = [slabs]
    if has_prev:
        per = tm // HIST_ROWS
        in_specs.append(pl.BlockSpec((N_POOL, HIST_ROWS, LANE),
                                     lambda b, i: (SLAB_U // N_POOL, jnp.maximum(rb(b, i) * per - 1, 0), 0)))
        args.append(slabs)
    in_specs += [
        pl.BlockSpec((1, HIST_ROWS, D_B), lambda b, i: (b, 0, 0)),
        pl.BlockSpec((None, N_POOL, POOL_GROUP, POOL_GROUP), lambda b, i: (layer, 0, 0, 0)),
        pl.BlockSpec((None, 1, D_B), lambda b, i: (layer, 0, 0)),
    ]
    args += [hist, w_pool, scale]
    return pl.pallas_call(
        functools.partial(_pool_kernel, pos0=pos0, has_prev=has_prev),
        grid=(batch, nt),
        in_specs=in_specs,
        out_specs=pl.BlockSpec((tm, D_B), lambda b, i: (ob(b, i), 0)),
        out_shape=jax.ShapeDtypeStruct((out_rows, D_B), BF16 if tm % PACK == 0 else F32),
        scratch_shapes=[pltpu.VMEM((N_POOL, HIST_ROWS + tm, LANE), F32)],
        compiler_params=_params("arbitrary", "arbitrary"),
        name="pool_mix",
    )(*args)


def _gla_kernel(q_ref, k_ref, v_ref, r_ref, ga_ref, w2_ref, bg_ref, gn_ref, s0_ref, o_ref, s_out_ref,
                st_ref, g_ref, *, chunk):
    i = pl.program_id(1)
    ts = q_ref.shape[1]

    @pl.when(i == 0)
    def _():
        st_ref[...] = s0_ref[0]

    z = jnp.dot(ga_ref[...], w2_ref[...], preferred_element_type=F32) + bg_ref[...]
    g_ref[...] = (jnp.minimum(z, 0.0) - jnp.log(1.0 + jnp.exp(-jnp.abs(z)))) / GATE_NORMALIZER

    ri = lax.broadcasted_iota(jnp.int32, (chunk, chunk), 0)
    ci = lax.broadcasted_iota(jnp.int32, (chunk, chunk), 1)
    tril = ci <= ri
    ones_tril = tril.astype(F32)
    qscale = DK_C ** -0.5
    nt = (((1,), (1,)), ((), ()))
    tn = (((0,), (0,)), ((), ()))

    def body(c, carry):
        rows = pl.ds(pl.multiple_of(c * chunk, chunk), chunk)
        b = jnp.dot(ones_tril, g_ref[rows, :], preferred_element_type=F32, precision=lax.Precision.HIGHEST)
        eb = jnp.exp(b)
        enb = jnp.exp(-b)
        bl = jnp.sum(g_ref[rows, :], axis=0, keepdims=True)
        ebl = jnp.exp(bl)
        ekd = jnp.exp(bl - b)
        for h in range(H_C):
            ks = slice(h * DK_P, (h + 1) * DK_P)
            vs = slice(h * DV_P, (h + 1) * DV_P)
            qh = q_ref[h, rows, :] * qscale
            kh = k_ref[h, rows, :]
            vh = jnp.concatenate([v_ref[VS * h + s, rows, :] for s in range(VS)], axis=-1)
            qe = qh * eb[:, ks]
            ke = kh * enb[:, ks]
            a = jnp.where(tril, lax.dot_general(qe, ke, nt, preferred_element_type=F32), 0.0)
            st = st_ref[h]
            o = jnp.dot(a, vh, preferred_element_type=F32)
            o = o + lax.dot_general(qe, st, nt, preferred_element_type=F32)
            kd = kh * ekd[:, ks]
            st_ref[h] = st * ebl[:, ks] + lax.dot_general(vh, kd, tn, preferred_element_type=F32)
            ms = jnp.sum(o * o, axis=-1, keepdims=True) * (1.0 / DV_C)
            on = o * lax.rsqrt(ms + EPS) * gn_ref[...]
            rh = jnp.concatenate([r_ref[VS * h + s, rows, :] for s in range(VS)], axis=-1)
            o_ref[rows, vs] = (on * (rh * (1.0 / (1.0 + jnp.exp(-rh))))).astype(o_ref.dtype)
        return carry

    lax.fori_loop(0, ts // chunk, body, 0)

    @pl.when(i == pl.num_programs(1) - 1)
    def _():
        s_out_ref[0] = st_ref[...]


def _gla(slabs, row0, out_rows, w2p, bgp, gnp, s0t, layer, batch, t):
    chunk = min(GLA_CHUNK, t)
    out_dtype = BF16 if chunk % PACK == 0 else F32
    ts = _row_tile(t, 512)
    nt = t // ts
    assert t % chunk == 0 and ts % chunk == 0 and row0 % ts == 0
    qk_w, v_w = H_C * DK_P, H_C * DV_P
    rb = lambda b, i: row0 // ts + b * nt + i
    grp = lambda n, s0: pl.BlockSpec((n, ts, LANE), lambda b, i: (s0 // n, rb(b, i), 0))
    assert SLAB_VC % (H_C * VS) == 0 and SLAB_RC % (H_C * VS) == 0 and SLAB_QC % H_C == 0 and SLAB_KC % H_C == 0
    par = lambda shape: pl.BlockSpec((None,) + shape, lambda b, i: (layer,) + (0,) * len(shape))
    state_spec = pl.BlockSpec((1, H_C, DV_P, DK_P), lambda b, i: (b, 0, 0, 0))
    return pl.pallas_call(
        functools.partial(_gla_kernel, chunk=chunk),
        grid=(batch, nt),
        in_specs=[grp(H_C, SLAB_QC), grp(H_C, SLAB_KC), grp(H_C * VS, SLAB_VC), grp(H_C * VS, SLAB_RC),
                  pl.BlockSpec((None, ts, LANE), lambda b, i: (SLAB_G, rb(b, i), 0)),
                  par((LANE, qk_w)), par((1, qk_w)), par((1, DV_P)), state_spec],
        out_specs=[pl.BlockSpec((ts, v_w), lambda b, i: (b * nt + i, 0)), state_spec],
        out_shape=[jax.ShapeDtypeStruct((out_rows, v_w), out_dtype),
                   jax.ShapeDtypeStruct((batch, H_C, DV_P, DK_P), F32)],
        scratch_shapes=[pltpu.VMEM((H_C, DV_P, DK_P), F32), pltpu.VMEM((ts, qk_w), F32)],
        compiler_params=_params("arbitrary", "arbitrary"),
        name="gla",
    )(slabs, slabs, slabs, slabs, slabs, w2p, bgp, gnp, s0t)


def _pad_head_rows(w, width, padded):
    depth, _, k = w.shape
    w = w.reshape(depth, H_C, width, k)
    w = jnp.pad(w, ((0, 0), (0, 0), (0, padded - width), (0, 0)))
    return w.reshape(depth, H_C * padded, k)


def _pad_heads(w, width, padded):
    lead = w.shape[:-1]
    w = w.reshape(lead + (H_C, width))
    w = jnp.pad(w, [(0, 0)] * len(lead) + [(0, 0), (0, padded - width)])
    return w.reshape(lead + (H_C * padded,))


def _relayout_weights(w_in, w_gate2, b_gate, gla_norm_g, w_o):
    wt = jnp.swapaxes(w_in, 1, 2)
    c0 = 3 * D_A + D_B
    qk = H_C * DK_C
    wq = wt[:, c0:c0 + qk]
    wk = wt[:, c0 + qk:c0 + 2 * qk]
    wv = wt[:, c0 + 2 * qk:c0 + 2 * qk + D_C]
    wr = wt[:, c0 + 2 * qk + D_C:c0 + 2 * qk + 2 * D_C]
    wg = wt[:, c0 + 2 * qk + 2 * D_C:]
    w_all = jnp.concatenate([
        _pad_head_rows(wv, DV_C, DV_P), _pad_head_rows(wr, DV_C, DV_P),
        _pad_head_rows(wq, DK_C, DK_P), _pad_head_rows(wk, DK_C, DK_P),
        wt[:, 3 * D_A:c0],
        jnp.pad(wg, ((0, 0), (0, 2 * LANE - GATE_RANK), (0, 0))),
        wt[:, :3 * D_A],
    ], axis=1).astype(BF16)
    assert w_all.shape[1] == N_SLABS * LANE
    w2p = jnp.pad(_pad_heads(w_gate2, DK_C, DK_P), ((0, 0), (0, LANE - GATE_RANK), (0, 0)))
    bgp = _pad_heads(b_gate, DK_C, DK_P)[:, None, :]
    gnp = jnp.pad(gla_norm_g, ((0, 0), (0, DV_P - DV_C)))[:, None, :]
    w_o_p = jnp.concatenate([w_o[:, :D_A + D_B], _pad_head_rows(w_o[:, D_A + D_B:], DV_C, DV_P)], axis=1)
    return w_all, w2p, bgp, gnp, w_o_p


def _state_to_kernel(s):
    s = jnp.swapaxes(s, 2, 3)
    return jnp.pad(s, ((0, 0), (0, 0), (0, DV_P - DV_C), (0, DK_P - DK_C)))


def _state_from_kernel(s):
    return jnp.swapaxes(s[:, :, :DV_C, :DK_C], 2, 3)


def _heads_to_rows(a):
    return jnp.transpose(a, (0, 1, 3, 2, 4))


def _set_rows(big, row0, small):
    return lax.dynamic_update_slice(big, small.astype(big.dtype), (row0, 0))


def kernel(x_prompt, x_sample, cache_win_k, cache_win_v, state_pool, state_gla, norm1_g, w_in, w_gate2, b_gate,
           w_pool, pool_scale, gla_norm_g, w_o, norm2_g, w_ffn_gate, w_ffn_up, w_ffn_down, final_norm_g):
    bp, sp, d_model = x_prompt.shape
    bs, ts, _ = x_sample.shape
    depth = w_in.shape[0]
    d_ff = w_ffn_down.shape[1]
    mp, ms = bp * sp, bs * ts
    m_all = mp + ms
    assert min(WIN_MAX, sp) == WIN_MAX
    assert d_ff % 2 == 0

    w_all, w2p, bgp, gnp, w_o_p = _relayout_weights(w_in, w_gate2, b_gate, gla_norm_g, w_o)
    cache_k = jnp.transpose(cache_win_k, (0, 1, 3, 2, 4))
    cache_v = jnp.transpose(cache_win_v, (0, 1, 3, 2, 4))
    pool_scale3 = pool_scale[:, None, :]

    x = jnp.concatenate([x_prompt.reshape(mp, d_model), x_sample.reshape(ms, d_model)], axis=0)
    zero_hist = jnp.zeros((bp, HIST_ROWS, D_B), F32)
    zero_state = jnp.zeros((bp, H_C, DV_P, DK_P), F32)
    p_k, p_v, p_pool, p_gla, s_k, s_v, s_pool, s_gla = [], [], [], [], [], [], [], []
    for l in range(depth):
        h = _rmsnorm(x, norm1_g[l], BF16)
        slabs = _proj(h, w_all, l)

        out_a, ka, va = _span_attn(slabs, bp, sp)
        out_b = _pool(slabs, 0, m_all, zero_hist, w_pool, pool_scale3, l, bp, sp, 0)
        out_c, st_p = _gla(slabs, 0, m_all, w2p, bgp, gnp, zero_state, l, bp, sp)
        p_k.append(ka)
        p_v.append(va)
        u_p = jnp.stack([slabs[SLAB_U:SLAB_U + N_POOL, (b + 1) * sp - POOL_BUF:(b + 1) * sp] for b in range(bp)])
        p_pool.append(jnp.transpose(u_p, (0, 2, 1, 3)).reshape(bp, POOL_BUF, D_B))
        p_gla.append(_state_from_kernel(st_p))

        hist = jnp.pad(state_pool[l], ((0, 0), (HIST_ROWS - POOL_BUF, 0), (0, 0)))
        sa = _dec_attn(slabs, mp, cache_k, cache_v, l, bs, ts)
        sb = _pool(slabs, mp, ms, hist, w_pool, pool_scale3, l, bs, ts, PAST_LEN)
        sc, st_s = _gla(slabs, mp, ms, w2p, bgp, gnp, _state_to_kernel(state_gla[l]), l, bs, ts)
        out_a = _set_rows(out_a, mp, sa)
        out_b = _set_rows(out_b, mp, sb)
        out_c = _set_rows(out_c, mp, sc)
        new_kv = slabs[SLAB_K:SLAB_K + 2 * H_A, mp:].reshape(2, H_A, bs, ts, HD_A)
        s_k.append(jnp.swapaxes(new_kv[0], 0, 1))
        s_v.append(jnp.swapaxes(new_kv[1], 0, 1))
        u_s = jnp.transpose(slabs[SLAB_U:SLAB_U + N_POOL, mp:].reshape(N_POOL, bs, ts, LANE), (1, 2, 0, 3))
        u_ext = jnp.concatenate([state_pool[l], u_s.reshape(bs, ts, D_B)], axis=1)
        s_pool.append(u_ext[:, -POOL_BUF:])
        s_gla.append(_state_from_kernel(st_s))

        x = _matmul_res([out_a, out_b, out_c], [(D_A, 0), (D_B, 0), (H_C * DV_P, 0)], w_o_p, l, 0, x, "out_proj")
        h2 = _rmsnorm(x, norm2_g[l], BF16)
        f = _glu(h2, w_ffn_gate, w_ffn_up, l)
        half = d_ff // 2
        x = _matmul_res([f], [(half, 0)], w_ffn_down, l, 0, x, "ffn_down")
        x = _matmul_res([f], [(half, 1)], w_ffn_down, l, half, x, "ffn_down")

    y_prompt = _rmsnorm(x, final_norm_g, F32, row0=0, rows=mp).reshape(bp, sp, d_model)
    y_sample = _rmsnorm(x, final_norm_g, F32, row0=mp, rows=ms).reshape(bs, ts, d_model)
    return (y_prompt, y_sample, _heads_to_rows(jnp.stack(p_k)), _heads_to_rows(jnp.stack(p_v)),
            jnp.stack(p_pool), jnp.stack(p_gla), _heads_to_rows(jnp.stack(s_k)), _heads_to_rows(jnp.stack(s_v)),
            jnp.stack(s_pool), jnp.stack(s_gla))
```

```python
import functools

import jax
import jax.numpy as jnp
from jax import lax
from jax.experimental import pallas as pl
from jax.experimental.pallas import tpu as pltpu

F32 = jnp.float32
BF16 = jnp.bfloat16

H_A, HD_A = 6, 128
D_A = H_A * HD_A
DILATED_PATTERNS = ((128, 1), (512, 4), (2048, 16))
WIN_MAX = 2048
POOL_WINDOWS = (2, 4, 8, 16)
N_POOL = 4
D_B = 512
POOL_GROUP = D_B // N_POOL
POOL_BUF = 15
H_C, DK_C, DV_C = 4, 96, 192
D_C = H_C * DV_C
GATE_RANK = 16
GATE_NORMALIZER = 16.0
GLA_CHUNK = 32
PAST_LEN = 16384
EPS = 1e-6
NEG_INF = -1e30

LANE = 128
SUBLANE = 8
PACK = 16
VMEM_LIMIT_BYTES = 56 * 1024 * 1024

DK_P = 128
DV_P = 256
HIST_ROWS = 16

VS = DV_P // LANE
SLAB_VC = 0
SLAB_RC = SLAB_VC + H_C * VS
SLAB_QC = SLAB_RC + H_C * VS
SLAB_KC = SLAB_QC + H_C
SLAB_U = SLAB_KC + H_C
SLAB_G = SLAB_U + N_POOL
SLAB_Q = SLAB_G + 2
SLAB_K = SLAB_Q + H_A
SLAB_V = SLAB_K + H_A
N_SLABS = SLAB_V + H_A
D_MIX_P = D_A + D_B + H_C * DV_P

ATTN_BLK = 128
assert all(w // d == ATTN_BLK and d & (d - 1) == 0 for w, d in DILATED_PATTERNS)

TM_ROWS = 1376
TM_NORM = 688
TN_PROJ = 768
TN_WIDE = 512
TM_RES = 688
TN_RES = 1024
GLA_GROUP = 4


def _params(*sem):
    return pltpu.CompilerParams(dimension_semantics=sem, vmem_limit_bytes=VMEM_LIMIT_BYTES)


def _row_tile(m, target, mult=PACK):
    if m <= target:
        return m
    t = target - target % mult
    while t >= mult:
        if m % t == 0:
            return t
        t -= mult
    raise ValueError(f"no row tile for {m}")


def _rmsnorm_kernel(x_ref, g_ref, o_ref):
    x = x_ref[...]
    ms = jnp.mean(x * x, axis=-1, keepdims=True)
    o_ref[...] = (x * lax.rsqrt(ms + EPS) * g_ref[...]).astype(o_ref.dtype)


def _rmsnorm(x, g, out_dtype, *, row0=0, rows=None):
    d = x.shape[1]
    rows = x.shape[0] if rows is None else rows
    tm = _row_tile(rows, TM_NORM)
    assert row0 % tm == 0
    return pl.pallas_call(
        _rmsnorm_kernel,
        grid=(rows // tm,),
        in_specs=[pl.BlockSpec((tm, d), lambda i: (i + row0 // tm, 0)), pl.BlockSpec((1, d), lambda i: (0, 0))],
        out_specs=pl.BlockSpec((tm, d), lambda i: (i, 0)),
        out_shape=jax.ShapeDtypeStruct((rows, d), out_dtype),
        compiler_params=_params("arbitrary"),
        name="rmsnorm",
    )(x, g.reshape(1, d))


def _proj_kernel(x_ref, w_ref, o_ref):
    acc = lax.dot_general(x_ref[...], w_ref[...], (((1,), (1,)), ((), ())), preferred_element_type=F32)
    for c in range(o_ref.shape[0]):
        o_ref[c] = acc[:, c * LANE:(c + 1) * LANE]


def _proj(x, w_all, layer):
    m, k = x.shape
    n = w_all.shape[1]
    tm = _row_tile(m, TM_ROWS)
    tn = TN_PROJ
    assert n % tn == 0
    return pl.pallas_call(
        _proj_kernel,
        grid=(n // tn, m // tm),
        in_specs=[pl.BlockSpec((tm, k), lambda j, i: (i, 0)),
                  pl.BlockSpec((None, tn, k), lambda j, i: (layer, j, 0))],
        out_specs=pl.BlockSpec((tn // LANE, tm, LANE), lambda j, i: (j, i, 0)),
        out_shape=jax.ShapeDtypeStruct((n // LANE, m, LANE), F32),
        compiler_params=_params("arbitrary", "arbitrary"),
        name="proj_in",
    )(x, w_all)


def _mm_res_kernel(*refs, x_counts):
    n_x, n_w = sum(x_counts), len(x_counts)
    xs = refs[:n_x]
    ws = refs[n_x:n_x + n_w]
    res_ref, o_ref = refs[n_x + n_w:n_x + n_w + 2]
    wbs = refs[n_x + n_w + 2:]

    @pl.when(pl.program_id(1) == 0)
    def _():
        for w_ref, wb_ref in zip(ws, wbs):
            wb_ref[...] = w_ref[...].astype(BF16)

    acc = res_ref[...]
    xi = 0
    for wb_ref, count in zip(wbs, x_counts):
        k0 = 0
        for x_ref in xs[xi:xi + count]:
            k = x_ref.shape[1]
            acc = acc + jnp.dot(x_ref[...], wb_ref[k0:k0 + k, :], preferred_element_type=F32)
            k0 += k
        xi += count
    o_ref[...] = acc


def _matmul_res(parts, layer, res, name):
    m, n = res.shape
    tm = _row_tile(m, TM_RES)
    tn = TN_RES
    assert n % tn == 0
    x_specs, w_specs, xs, ws, scratch = [], [], [], [], []
    for w, rblk, x_list in parts:
        ktot = sum(width for _, width, _ in x_list)
        for x, width, cb in x_list:
            xs.append(x)
            x_specs.append(pl.BlockSpec((tm, width), lambda j, i, cb=cb: (i, cb)))
        ws.append(w)
        w_specs.append(pl.BlockSpec((None, ktot, tn), lambda j, i, rblk=rblk: (layer, rblk, j)))
        scratch.append(pltpu.VMEM((ktot, tn), BF16))
    return pl.pallas_call(
        functools.partial(_mm_res_kernel, x_counts=tuple(len(p[2]) for p in parts)),
        grid=(n // tn, m // tm),
        in_specs=x_specs + w_specs + [pl.BlockSpec((tm, tn), lambda j, i: (i, j))],
        out_specs=pl.BlockSpec((tm, tn), lambda j, i: (i, j)),
        out_shape=jax.ShapeDtypeStruct((m, n), F32),
        scratch_shapes=scratch,
        input_output_aliases={len(xs) + len(ws): 0},
        compiler_params=_params("arbitrary", "arbitrary"),
        name=name,
    )(*xs, *ws, res)


def _glu_kernel(x_ref, wg_ref, wu_ref, o_ref, wgb_ref, wub_ref, *, row_split):
    @pl.when(pl.program_id(1) == 0)
    def _():
        wgb_ref[...] = wg_ref[...].astype(BF16)
        wub_ref[...] = wu_ref[...].astype(BF16)

    rows = x_ref.shape[0] // row_split
    for r in range(row_split):
        rs = slice(r * rows, (r + 1) * rows)
        x = x_ref[rs, :]
        g = jnp.dot(x, wgb_ref[...], preferred_element_type=F32)
        u = jnp.dot(x, wub_ref[...], preferred_element_type=F32)
        o_ref[rs, :] = (g * (1.0 / (1.0 + jnp.exp(-g))) * u).astype(o_ref.dtype)


def _glu(x, wg, wu, layer):
    m, k = x.shape
    n = wg.shape[2]
    tm = _row_tile(m, TM_ROWS)
    tn = TN_WIDE
    row_split = 2 if tm % (2 * PACK) == 0 else 1
    assert n % tn == 0
    w_spec = pl.BlockSpec((None, k, tn), lambda j, i: (layer, 0, j))
    return pl.pallas_call(
        functools.partial(_glu_kernel, row_split=row_split),
        grid=(n // tn, m // tm),
        in_specs=[pl.BlockSpec((tm, k), lambda j, i: (i, 0)), w_spec, w_spec],
        out_specs=pl.BlockSpec((tm, tn), lambda j, i: (i, j)),
        out_shape=jax.ShapeDtypeStruct((m, n), BF16),
        scratch_shapes=[pltpu.VMEM((k, tn), BF16), pltpu.VMEM((k, tn), BF16)],
        compiler_params=_params("arbitrary", "arbitrary"),
        name="ffn_glu",
    )(x, wg, wu)


def _span_attn_kernel(q_ref, k_ref, v_ref, *rest, skew, n_carried):
    o_ref, pk_ref, pv_ref, k_scr, v_scr, o_scr, l_scr = rest[n_carried:]
    n0 = pl.program_id(2)
    span = q_ref.shape[0]
    blk = ATTN_BLK
    units = span // blk
    scale = HD_A ** -0.5
    cur = (n0 % 2) * span
    prev = span - cur

    @pl.when(n0 == 0)
    def _():
        k_scr[pl.ds(prev, span), :] = jnp.zeros((span, HD_A), F32)
        v_scr[pl.ds(prev, span), :] = jnp.zeros((span, HD_A), F32)

    k_all = k_ref[...]
    v_all = v_ref[...]
    k_scr[pl.ds(cur, span), :] = k_all
    v_scr[pl.ds(cur, span), :] = v_all

    @pl.when(n0 == pl.num_programs(2) - 1)
    def _():
        pk_ref[...] = k_all
        pv_ref[...] = v_all

    qi = lax.broadcasted_iota(jnp.int32, (blk, blk), 0)
    ki = lax.broadcasted_iota(jnp.int32, (blk, blk), 1)
    mask_cur = ki <= qi
    nt = (((1,), (1,)), ((), ()))

    def rows(start, dil):
        return pl.ds(start, blk) if dil == 1 else pl.ds(start, blk, stride=dil)

    mask_prev_first = ki >= qi + jnp.where(n0 > 0, 0, blk)
    mask_prev_later = ki >= qi

    def unit_starts(u, dil):
        r, nb = u % dil, u // dil
        start = r + dil * blk * nb
        prev_start = cur + start - dil * blk if nb > 0 else prev + r + span - dil * blk
        return start, prev_start, nb

    def scores(u, dil):
        start, prev_start, nb = unit_starts(u, dil)
        q = q_ref[rows(start, dil), :].astype(BF16)
        kc = k_scr[rows(cur + start, dil), :].astype(BF16)
        kp = k_scr[rows(prev_start, dil), :].astype(BF16)
        sp = lax.dot_general(q, kp, nt, preferred_element_type=F32) * scale
        sc = lax.dot_general(q, kc, nt, preferred_element_type=F32) * scale
        sp = jnp.where(mask_prev_later if nb > 0 else mask_prev_first, sp, NEG_INF)
        sc = jnp.where(mask_cur, sc, NEG_INF)
        return sp, sc

    def finish(p, u, dil, sp, sc):
        start, prev_start, _ = unit_starts(u, dil)
        vc = v_scr[rows(cur + start, dil), :].astype(BF16)
        vp = v_scr[rows(prev_start, dil), :].astype(BF16)
        m = jnp.max(jnp.maximum(sp, sc), axis=-1, keepdims=True)
        pp = jnp.exp(sp - m)
        pc = jnp.exp(sc - m)
        den = jnp.sum(pp + pc, axis=-1, keepdims=True)
        o = jnp.dot(pp.astype(BF16), vp, preferred_element_type=F32)
        o = o + jnp.dot(pc.astype(BF16), vc, preferred_element_type=F32)
        o_scr[p, rows(start, dil), :] = o / den
        l_scr[p, rows(start, dil), :] = jnp.broadcast_to(m + jnp.log(den), (blk, HD_A))

    order = [(p, u, dil) for p, (_, dil) in enumerate(DILATED_PATTERNS) for u in range(units)]
    pending = []
    for step in range(len(order) + skew):
        if step < len(order):
            p, u, dil = order[step]
            pending.append((p, u, dil) + scores(u, dil))
        if step >= skew:
            finish(*pending.pop(0))

    ct = 64

    def combine(c, carry):
        rs = pl.ds(pl.multiple_of(c * ct, ct), ct)
        l1, l2, l3 = l_scr[0, rs, :], l_scr[1, rs, :], l_scr[2, rs, :]
        mx = jnp.maximum(jnp.maximum(l1, l2), l3)
        e1, e2, e3 = jnp.exp(l1 - mx), jnp.exp(l2 - mx), jnp.exp(l3 - mx)
        num = e1 * o_scr[0, rs, :] + e2 * o_scr[1, rs, :] + e3 * o_scr[2, rs, :]
        o_ref[rs, :] = (num / (e1 + e2 + e3)).astype(o_ref.dtype)
        return carry

    lax.fori_loop(0, span // ct, combine, 0, unroll=2)


def _span_attn(slabs, batch, seq, layer, depth, win_k, win_v):
    span = WIN_MAX
    assert seq % span == 0 and len(DILATED_PATTERNS) == 3
    nsp = seq // span
    slab = lambda s0: pl.BlockSpec((None, span, HD_A), lambda b, h, n: (s0 + h, b * nsp + n, 0))
    win_spec = pl.BlockSpec((None, None, None, span, HD_A), lambda b, h, n: (layer, b, h, 0, 0))
    win_shape = jax.ShapeDtypeStruct((depth, batch, H_A, span, HD_A), F32)
    in_specs = [slab(SLAB_Q), slab(SLAB_K), slab(SLAB_V)]
    args = [slabs, slabs, slabs]
    aliases = {}
    if win_k is not None:
        in_specs += [pl.BlockSpec(memory_space=pl.ANY)] * 2
        args += [win_k, win_v]
        aliases = {3: 1, 4: 2}
    return pl.pallas_call(
        functools.partial(_span_attn_kernel, skew=4, n_carried=len(aliases)),
        grid=(batch, H_A, nsp),
        in_specs=in_specs,
        out_specs=[pl.BlockSpec((span, HD_A), lambda b, h, n: (b * nsp + n, h)), win_spec, win_spec],
        out_shape=[jax.ShapeDtypeStruct((slabs.shape[1], D_A), BF16), win_shape, win_shape],
        scratch_shapes=[pltpu.VMEM((2 * span, HD_A), F32), pltpu.VMEM((2 * span, HD_A), F32),
                        pltpu.VMEM((3, span, HD_A), F32), pltpu.VMEM((3, span, HD_A), F32)],
        input_output_aliases=aliases,
        compiler_params=_params("arbitrary", "arbitrary", "arbitrary"),
        name="span_attn",
    )(*args)


def _dec_attn_kernel(q_ref, kn_ref, vn_ref, kc_ref, vc_ref, o_ref, *, past):
    t = q_ref.shape[1]
    scale = HD_A ** -0.5
    nt = (((1,), (1,)), ((), ()))
    d_c = past + lax.broadcasted_iota(jnp.int32, (t, past), 0) - lax.broadcasted_iota(jnp.int32, (t, past), 1)
    d_n = lax.broadcasted_iota(jnp.int32, (t, t), 0) - lax.broadcasted_iota(jnp.int32, (t, t), 1)
    masks = []
    for window, dil in DILATED_PATTERNS:
        m_c = jnp.logical_and((d_c & (dil - 1)) == 0, d_c <= window)
        m_n = jnp.logical_and(jnp.logical_and(d_n >= 0, (d_n & (dil - 1)) == 0), d_n <= window)
        masks.append((m_c, m_n))
    for h in range(H_A):
        q = q_ref[h]
        vc = vc_ref[h]
        vn = vn_ref[h]
        s_c = lax.dot_general(q, kc_ref[h], nt, preferred_element_type=F32) * scale
        s_n = lax.dot_general(q, kn_ref[h], nt, preferred_element_type=F32) * scale
        outs, lses = [], []
        for m_c, m_n in masks:
            a_c = jnp.where(m_c, s_c, NEG_INF)
            a_n = jnp.where(m_n, s_n, NEG_INF)
            m = jnp.maximum(jnp.max(a_c, axis=-1, keepdims=True), jnp.max(a_n, axis=-1, keepdims=True))
            p_c = jnp.exp(a_c - m)
            p_n = jnp.exp(a_n - m)
            den = jnp.sum(p_c, axis=-1, keepdims=True) + jnp.sum(p_n, axis=-1, keepdims=True)
            o = jnp.dot(p_c, vc, preferred_element_type=F32) + jnp.dot(p_n, vn, preferred_element_type=F32)
            outs.append(o / den)
            lses.append(m + jnp.log(den))
        mx = jnp.maximum(jnp.maximum(lses[0], lses[1]), lses[2])
        es = [jnp.exp(l - mx) for l in lses]
        num = es[0] * outs[0] + es[1] * outs[1] + es[2] * outs[2]
        o_ref[:, h * HD_A:(h + 1) * HD_A] = (num / (es[0] + es[1] + es[2])).astype(o_ref.dtype)


def _dec_attn(slabs, row0, cache_k, cache_v, layer, batch, t):
    past = cache_k.shape[3]
    assert row0 % t == 0 and SLAB_Q % H_A == 0 and SLAB_K % H_A == 0 and SLAB_V % H_A == 0
    new_spec = lambda s0: pl.BlockSpec((H_A, t, HD_A), lambda b: (s0 // H_A, row0 // t + b, 0))
    cache_spec = pl.BlockSpec((None, None, H_A, past, HD_A), lambda b: (layer, b, 0, 0, 0))
    return pl.pallas_call(
        functools.partial(_dec_attn_kernel, past=past),
        grid=(batch,),
        in_specs=[new_spec(SLAB_Q), new_spec(SLAB_K), new_spec(SLAB_V), cache_spec, cache_spec],
        out_specs=pl.BlockSpec((t, D_A), lambda b: (b, 0)),
        out_shape=jax.ShapeDtypeStruct((batch * t, D_A), F32),
        compiler_params=_params("arbitrary"),
        name="dec_attn",
    )(slabs, slabs, slabs, cache_k, cache_v)


def _pool_kernel(*refs, pos0, has_prev):
    if has_prev:
        u_ref, prev_ref, hist_ref, w_ref, sc_ref, o_ref, ext_ref = refs
    else:
        u_ref, hist_ref, w_ref, sc_ref, o_ref, ext_ref = refs
        prev_ref = None
    i = pl.program_id(1)
    tm = u_ref.shape[1]

    @pl.when(i == 0)
    def _():
        for gi in range(N_POOL):
            ext_ref[gi, 0:HIST_ROWS, :] = hist_ref[0, :, gi * POOL_GROUP:(gi + 1) * POOL_GROUP]

    if has_prev:
        @pl.when(i > 0)
        def _():
            ext_ref[:, 0:HIST_ROWS, :] = prev_ref[...]

    ext_ref[:, HIST_ROWS:HIST_ROWS + tm, :] = u_ref[...]
    pos = pos0 + i * tm + lax.broadcasted_iota(jnp.int32, (tm, POOL_GROUP), 0)
    for gi, w in enumerate(POOL_WINDOWS):
        cs = slice(gi * POOL_GROUP, (gi + 1) * POOL_GROUP)
        tok = ext_ref[gi, HIST_ROWS:HIST_ROWS + tm, :]
        tot = tok
        for back in range(1, w):
            tot = tot + ext_ref[gi, HIST_ROWS - back:HIST_ROWS - back + tm, :]
        cnt = jnp.minimum(w, pos + 1).astype(F32)
        pooled = tot / cnt - tok
        out = jnp.dot(pooled.astype(BF16), w_ref[gi].astype(BF16), preferred_element_type=F32)
        o_ref[:, cs] = (out * sc_ref[:, cs]).astype(o_ref.dtype)


def _pool(slabs, row0, out_rows, hist, w_pool, scale, layer, batch, t, pos0):
    tm = _row_tile(t, 512)
    nt = t // tm
    has_prev = nt > 1
    assert row0 % tm == 0
    rb = lambda b, i: row0 // tm + b * nt + i
    ob = lambda b, i: b * nt + i
    in_specs = [pl.BlockSpec((N_POOL, tm, LANE), lambda b, i: (SLAB_U // N_POOL, rb(b, i), 0))]
    args = [slabs]
    if has_prev:
        per = tm // HIST_ROWS
        in_specs.append(pl.BlockSpec((N_POOL, HIST_ROWS, LANE),
                                     lambda b, i: (SLAB_U // N_POOL, jnp.maximum(rb(b, i) * per - 1, 0), 0)))
        args.append(slabs)
    in_specs += [
        pl.BlockSpec((1, HIST_ROWS, D_B), lambda b, i: (b, 0, 0)),
        pl.BlockSpec((None, N_POOL, POOL_GROUP, POOL_GROUP), lambda b, i: (layer, 0, 0, 0)),
        pl.BlockSpec((None, 1, D_B), lambda b, i: (layer, 0, 0)),
    ]
    args += [hist, w_pool, scale]
    return pl.pallas_call(
        functools.partial(_pool_kernel, pos0=pos0, has_prev=has_prev),
        grid=(batch, nt),
        in_specs=in_specs,
        out_specs=pl.BlockSpec((tm, D_B), lambda b, i: (ob(b, i), 0)),
        out_shape=jax.ShapeDtypeStruct((out_rows, D_B), BF16 if tm % PACK == 0 else F32),
        scratch_shapes=[pltpu.VMEM((N_POOL, HIST_ROWS + tm, LANE), F32)],
        compiler_params=_params("arbitrary", "arbitrary"),
        name="pool_mix",
    )(*args)


def _gla_kernel(q_ref, k_ref, v_ref, r_ref, ga_ref, w2_ref, bg_ref, gn_ref, s0_ref, o_ref, s_out_ref,
                st_ref, g_ref, *, chunk, group):
    i = pl.program_id(1)
    ts = q_ref.shape[1]

    @pl.when(i == 0)
    def _():
        st_ref[...] = s0_ref[0]

    z = jnp.dot(ga_ref[...], w2_ref[...], preferred_element_type=F32) + bg_ref[...]
    g_ref[...] = (jnp.minimum(z, 0.0) - jnp.log(1.0 + jnp.exp(-jnp.abs(z)))) / GATE_NORMALIZER

    gr = group * chunk
    ri = lax.broadcasted_iota(jnp.int32, (gr, gr), 0)
    ci = lax.broadcasted_iota(jnp.int32, (gr, gr), 1)
    same_chunk = (ri // chunk) == (ci // chunk)
    tril = jnp.logical_and(same_chunk, ci <= ri)
    cum_mat = tril.astype(F32)
    tot_mat = same_chunk.astype(F32)
    qscale = DK_C ** -0.5
    nt = (((1,), (1,)), ((), ()))
    tn = (((0,), (0,)), ((), ()))
    hi = lax.Precision.HIGHEST

    def body(gi, carry):
        rows = pl.ds(pl.multiple_of(gi * gr, gr), gr)
        gg = g_ref[rows, :]
        b = jnp.dot(cum_mat, gg, preferred_element_type=F32, precision=hi)
        bl = jnp.dot(tot_mat, gg, preferred_element_type=F32, precision=hi)
        eb = jnp.exp(b)
        enb = jnp.exp(-b)
        ebl = jnp.exp(bl)
        ekd = jnp.exp(bl - b)
        for h in range(H_C):
            ks = slice(h * DK_P, (h + 1) * DK_P)
            vs = slice(h * DV_P, (h + 1) * DV_P)
            qh = q_ref[h, rows, :] * qscale
            kh = k_ref[h, rows, :]
            vh = jnp.concatenate([v_ref[VS * h + s, rows, :] for s in range(VS)], axis=-1).astype(BF16)
            qe = (qh * eb[:, ks]).astype(BF16)
            ke = (kh * enb[:, ks]).astype(BF16)
            kd = (kh * ekd[:, ks]).astype(BF16)
            a = lax.dot_general(qe, ke, nt, preferred_element_type=F32)
            kv = [lax.dot_general(vh[c * chunk:(c + 1) * chunk], kd[c * chunk:(c + 1) * chunk], tn,
                                  preferred_element_type=F32) for c in range(group)]
            st = st_ref[h]
            inter = []
            for c in range(group):
                inter.append(lax.dot_general(qe[c * chunk:(c + 1) * chunk], st.astype(BF16), nt,
                                             preferred_element_type=F32))
                st = st * ebl[c * chunk:c * chunk + 1, ks] + kv[c]
            st_ref[h] = st
            a = jnp.where(tril, a, 0.0).astype(BF16)
            o = jnp.dot(a, vh, preferred_element_type=F32)
            o = o + (inter[0] if group == 1 else jnp.concatenate(inter, axis=0))
            ms = jnp.sum(o * o, axis=-1, keepdims=True) * (1.0 / DV_C)
            on = o * lax.rsqrt(ms + EPS) * gn_ref[...]
            rh = jnp.concatenate([r_ref[VS * h + s, rows, :] for s in range(VS)], axis=-1)
            o_ref[rows, vs] = (on * (rh * (1.0 / (1.0 + jnp.exp(-rh))))).astype(o_ref.dtype)
        return carry

    lax.fori_loop(0, ts // gr, body, 0)

    @pl.when(i == pl.num_programs(1) - 1)
    def _():
        s_out_ref[0] = st_ref[...]


def _gla(slabs, row0, out_rows, w2p, bgp, gnp, s0t, layer, batch, t):
    chunk = min(GLA_CHUNK, t)
    out_dtype = BF16 if chunk % PACK == 0 else F32
    ts = _row_tile(t, 512)
    nt = t // ts
    group = GLA_GROUP if ts % (GLA_GROUP * chunk) == 0 else 1
    assert t % chunk == 0 and ts % chunk == 0 and row0 % ts == 0
    qk_w, v_w = H_C * DK_P, H_C * DV_P
    rb = lambda b, i: row0 // ts + b * nt + i
    grp = lambda n, s0: pl.BlockSpec((n, ts, LANE), lambda b, i: (s0 // n, rb(b, i), 0))
    assert SLAB_VC % (H_C * VS) == 0 and SLAB_RC % (H_C * VS) == 0 and SLAB_QC % H_C == 0 and SLAB_KC % H_C == 0
    par = lambda shape: pl.BlockSpec((None,) + shape, lambda b, i: (layer,) + (0,) * len(shape))
    state_spec = pl.BlockSpec((1, H_C, DV_P, DK_P), lambda b, i: (b, 0, 0, 0))
    return pl.pallas_call(
        functools.partial(_gla_kernel, chunk=chunk, group=group),
        grid=(batch, nt),
        in_specs=[grp(H_C, SLAB_QC), grp(H_C, SLAB_KC), grp(H_C * VS, SLAB_VC), grp(H_C * VS, SLAB_RC),
                  pl.BlockSpec((None, ts, LANE), lambda b, i: (SLAB_G, rb(b, i), 0)),
                  par((LANE, qk_w)), par((1, qk_w)), par((1, DV_P)), state_spec],
        out_specs=[pl.BlockSpec((ts, v_w), lambda b, i: (b * nt + i, 0)), state_spec],
        out_shape=[jax.ShapeDtypeStruct((out_rows, v_w), out_dtype),
                   jax.ShapeDtypeStruct((batch, H_C, DV_P, DK_P), F32)],
        scratch_shapes=[pltpu.VMEM((H_C, DV_P, DK_P), F32), pltpu.VMEM((ts, qk_w), F32)],
        compiler_params=_params("arbitrary", "arbitrary"),
        name="gla",
    )(slabs, slabs, slabs, slabs, slabs, w2p, bgp, gnp, s0t)


def _pad_head_rows(w, width, padded):
    depth, _, k = w.shape
    w = w.reshape(depth, H_C, width, k)
    w = jnp.pad(w, ((0, 0), (0, 0), (0, padded - width), (0, 0)))
    return w.reshape(depth, H_C * padded, k)


def _pad_heads(w, width, padded):
    lead = w.shape[:-1]
    w = w.reshape(lead + (H_C, width))
    w = jnp.pad(w, [(0, 0)] * len(lead) + [(0, 0), (0, padded - width)])
    return w.reshape(lead + (H_C * padded,))


def _relayout_weights(w_in, w_gate2, b_gate, gla_norm_g, w_o):
    wt = jnp.swapaxes(w_in, 1, 2)
    c0 = 3 * D_A + D_B
    qk = H_C * DK_C
    wq = wt[:, c0:c0 + qk]
    wk = wt[:, c0 + qk:c0 + 2 * qk]
    wv = wt[:, c0 + 2 * qk:c0 + 2 * qk + D_C]
    wr = wt[:, c0 + 2 * qk + D_C:c0 + 2 * qk + 2 * D_C]
    wg = wt[:, c0 + 2 * qk + 2 * D_C:]
    w_all = jnp.concatenate([
        _pad_head_rows(wv, DV_C, DV_P), _pad_head_rows(wr, DV_C, DV_P),
        _pad_head_rows(wq, DK_C, DK_P), _pad_head_rows(wk, DK_C, DK_P),
        wt[:, 3 * D_A:c0],
        jnp.pad(wg, ((0, 0), (0, 2 * LANE - GATE_RANK), (0, 0))),
        wt[:, :3 * D_A],
    ], axis=1).astype(BF16)
    assert w_all.shape[1] == N_SLABS * LANE
    w2p = jnp.pad(_pad_heads(w_gate2, DK_C, DK_P), ((0, 0), (0, LANE - GATE_RANK), (0, 0)))
    bgp = _pad_heads(b_gate, DK_C, DK_P)[:, None, :]
    gnp = jnp.pad(gla_norm_g, ((0, 0), (0, DV_P - DV_C)))[:, None, :]
    wo_c = _pad_head_rows(w_o[:, D_A + D_B:], DV_C, DV_P)
    return w_all, w2p, bgp, gnp, wo_c


def _state_to_kernel(s):
    s = jnp.swapaxes(s, 2, 3)
    return jnp.pad(s, ((0, 0), (0, 0), (0, DV_P - DV_C), (0, DK_P - DK_C)))


def _state_from_kernel(s):
    return jnp.swapaxes(s[:, :, :DV_C, :DK_C], 2, 3)


def _heads_to_rows(a):
    return jnp.transpose(a, (0, 1, 3, 2, 4))


def _set_rows(big, row0, small):
    return lax.dynamic_update_slice(big, small.astype(big.dtype), (row0, 0))


def kernel(x_prompt, x_sample, cache_win_k, cache_win_v, state_pool, state_gla, norm1_g, w_in, w_gate2, b_gate,
           w_pool, pool_scale, gla_norm_g, w_o, norm2_g, w_ffn_gate, w_ffn_up, w_ffn_down, final_norm_g):
    bp, sp, d_model = x_prompt.shape
    bs, ts, _ = x_sample.shape
    depth = w_in.shape[0]
    d_ff = w_ffn_down.shape[1]
    mp, ms = bp * sp, bs * ts
    m_all = mp + ms
    assert min(WIN_MAX, sp) == WIN_MAX
    assert d_ff % 2 == 0

    w_all, w2p, bgp, gnp, wo_c = _relayout_weights(w_in, w_gate2, b_gate, gla_norm_g, w_o)
    cache_k = jnp.transpose(cache_win_k, (0, 1, 3, 2, 4))
    cache_v = jnp.transpose(cache_win_v, (0, 1, 3, 2, 4))
    pool_scale3 = pool_scale[:, None, :]

    x = jnp.concatenate([x_prompt.reshape(mp, d_model), x_sample.reshape(ms, d_model)], axis=0)
    zero_hist = jnp.zeros((bp, HIST_ROWS, D_B), F32)
    zero_state = jnp.zeros((bp, H_C, DV_P, DK_P), F32)
    p_pool, p_gla, s_k, s_v, s_pool, s_gla = [], [], [], [], [], []
    win_k = win_v = None
    for l in range(depth):
        h = _rmsnorm(x, norm1_g[l], BF16)
        slabs = _proj(h, w_all, l)

        out_a, win_k, win_v = _span_attn(slabs, bp, sp, l, depth, win_k, win_v)
        out_b = _pool(slabs, 0, m_all, zero_hist, w_pool, pool_scale3, l, bp, sp, 0)
        out_c, st_p = _gla(slabs, 0, m_all, w2p, bgp, gnp, zero_state, l, bp, sp)
        u_p = jnp.stack([slabs[SLAB_U:SLAB_U + N_POOL, (b + 1) * sp - POOL_BUF:(b + 1) * sp] for b in range(bp)])
        p_pool.append(jnp.transpose(u_p, (0, 2, 1, 3)).reshape(bp, POOL_BUF, D_B))
        p_gla.append(_state_from_kernel(st_p))

        hist = jnp.pad(state_pool[l], ((0, 0), (HIST_ROWS - POOL_BUF, 0), (0, 0)))
        sa = _dec_attn(slabs, mp, cache_k, cache_v, l, bs, ts)
        sb = _pool(slabs, mp, ms, hist, w_pool, pool_scale3, l, bs, ts, PAST_LEN)
        sc, st_s = _gla(slabs, mp, ms, w2p, bgp, gnp, _state_to_kernel(state_gla[l]), l, bs, ts)
        out_a = _set_rows(out_a, mp, sa)
        out_b = _set_rows(out_b, mp, sb)
        out_c = _set_rows(out_c, mp, sc)
        new_kv = slabs[SLAB_K:SLAB_K + 2 * H_A, mp:].reshape(2, H_A, bs, ts, HD_A)
        s_k.append(jnp.swapaxes(new_kv[0], 0, 1))
        s_v.append(jnp.swapaxes(new_kv[1], 0, 1))
        u_s = jnp.transpose(slabs[SLAB_U:SLAB_U + N_POOL, mp:].reshape(N_POOL, bs, ts, LANE), (1, 2, 0, 3))
        u_ext = jnp.concatenate([state_pool[l], u_s.reshape(bs, ts, D_B)], axis=1)
        s_pool.append(u_ext[:, -POOL_BUF:])
        s_gla.append(_state_from_kernel(st_s))

        x = _matmul_res([(w_o, 0, [(out_a, D_A, 0), (out_b, D_B, 0)]), (wo_c, 0, [(out_c, H_C * DV_P, 0)])],
                        l, x, "out_proj")
        h2 = _rmsnorm(x, norm2_g[l], BF16)
        f = _glu(h2, w_ffn_gate, w_ffn_up, l)
        half = d_ff // 2
        x = _matmul_res([(w_ffn_down, 0, [(f, half, 0)])], l, x, "ffn_down")
        x = _matmul_res([(w_ffn_down, 1, [(f, half, 1)])], l, x, "ffn_down")

    y_prompt = _rmsnorm(x, final_norm_g, F32, row0=0, rows=mp).reshape(bp, sp, d_model)
    y_sample = _rmsnorm(x, final_norm_g, F32, row0=mp, rows=ms).reshape(bs, ts, d_model)
    return (y_prompt, y_sample, _heads_to_rows(win_k), _heads_to_rows(win_v),
            jnp.stack(p_pool), jnp.stack(p_gla), _heads_to_rows(jnp.stack(s_k)), _heads_to_rows(jnp.stack(s_v)),
            jnp.stack(s_pool), jnp.stack(s_gla))
```

```python
import functools

import jax
import jax.numpy as jnp
from jax import lax
from jax.experimental import pallas as pl
from jax.experimental.pallas import tpu as pltpu

F32 = jnp.float32
BF16 = jnp.bfloat16

H_A, HD_A = 6, 128
D_A = H_A * HD_A
DILATED_PATTERNS = ((128, 1), (512, 4), (2048, 16))
WIN_MAX = 2048
POOL_WINDOWS = (2, 4, 8, 16)
N_POOL = 4
D_B = 512
POOL_GROUP = D_B // N_POOL
POOL_BUF = 15
H_C, DK_C, DV_C = 4, 96, 192
D_C = H_C * DV_C
GATE_RANK = 16
GATE_NORMALIZER = 16.0
GLA_CHUNK = 32
PAST_LEN = 16384
EPS = 1e-6
NEG_INF = -1e30
LOG2E = 1.4426950408889634

LANE = 128
SUBLANE = 8
PACK = 16
VMEM_LIMIT_BYTES = 56 * 1024 * 1024

DK_P = 128
DV_P = 256
HIST_ROWS = 16

VS = DV_P // LANE
SLAB_VC = 0
SLAB_RC = SLAB_VC + H_C * VS
SLAB_QC = SLAB_RC + H_C * VS
SLAB_KC = SLAB_QC + H_C
SLAB_U = SLAB_KC + H_C
SLAB_G = SLAB_U + N_POOL
SLAB_Q = SLAB_G + 2
SLAB_K = SLAB_Q + H_A
SLAB_V = SLAB_K + H_A
N_SLABS = SLAB_V + H_A
D_MIX_P = D_A + D_B + H_C * DV_P

ATTN_BLK = 128
assert all(w // d == ATTN_BLK and d & (d - 1) == 0 for w, d in DILATED_PATTERNS)

TM_ROWS = 1376
TM_NORM = 688
TN_PROJ = 768
TN_WIDE = 512
TM_RES = 688
TN_RES = 1024
GLA_GROUP = 4


def _params(*sem):
    return pltpu.CompilerParams(dimension_semantics=sem, vmem_limit_bytes=VMEM_LIMIT_BYTES)


def _row_tile(m, target, mult=PACK):
    if m <= target:
        return m
    t = target - target % mult
    while t >= mult:
        if m % t == 0:
            return t
        t -= mult
    raise ValueError(f"no row tile for {m}")


def _rmsnorm_kernel(x_ref, g_ref, o_ref):
    x = x_ref[...]
    ms = jnp.mean(x * x, axis=-1, keepdims=True)
    o_ref[...] = (x * lax.rsqrt(ms + EPS) * g_ref[...]).astype(o_ref.dtype)


def _rmsnorm(x, g, out_dtype, *, row0=0, rows=None):
    d = x.shape[1]
    rows = x.shape[0] if rows is None else rows
    tm = _row_tile(rows, TM_NORM)
    assert row0 % tm == 0
    return pl.pallas_call(
        _rmsnorm_kernel,
        grid=(rows // tm,),
        in_specs=[pl.BlockSpec((tm, d), lambda i: (i + row0 // tm, 0)), pl.BlockSpec((1, d), lambda i: (0, 0))],
        out_specs=pl.BlockSpec((tm, d), lambda i: (i, 0)),
        out_shape=jax.ShapeDtypeStruct((rows, d), out_dtype),
        compiler_params=_params("arbitrary"),
        name="rmsnorm",
    )(x, g.reshape(1, d))


def _proj_kernel(x_ref, g_ref, w_ref, o_ref, h_ref):
    @pl.when(pl.program_id(1) == 0)
    def _():
        x = x_ref[...]
        ms = jnp.mean(x * x, axis=-1, keepdims=True)
        h_ref[...] = (x * lax.rsqrt(ms + EPS) * g_ref[...]).astype(BF16)

    acc = lax.dot_general(h_ref[...], w_ref[...], (((1,), (1,)), ((), ())), preferred_element_type=F32)
    for c in range(o_ref.shape[0]):
        o_ref[c] = acc[:, c * LANE:(c + 1) * LANE]


def _proj(x, g, w_all, layer):
    m, k = x.shape
    n = w_all.shape[1]
    tm = _row_tile(m, TM_ROWS)
    tn = TN_PROJ
    assert n % tn == 0
    return pl.pallas_call(
        _proj_kernel,
        grid=(m // tm, n // tn),
        in_specs=[pl.BlockSpec((tm, k), lambda i, j: (i, 0)),
                  pl.BlockSpec((1, k), lambda i, j: (0, 0)),
                  pl.BlockSpec((None, tn, k), lambda i, j: (layer, j, 0))],
        out_specs=pl.BlockSpec((tn // LANE, tm, LANE), lambda i, j: (j, i, 0)),
        out_shape=jax.ShapeDtypeStruct((n // LANE, m, LANE), F32),
        scratch_shapes=[pltpu.VMEM((tm, k), BF16)],
        compiler_params=_params("arbitrary", "arbitrary"),
        name="proj_in",
    )(x, g.reshape(1, k), w_all)


def _mm_res_kernel(*refs, x_counts):
    n_x, n_w = sum(x_counts), len(x_counts)
    xs = refs[:n_x]
    ws = refs[n_x:n_x + n_w]
    res_ref, o_ref = refs[n_x + n_w:n_x + n_w + 2]
    wbs = refs[n_x + n_w + 2:]

    @pl.when(pl.program_id(1) == 0)
    def _():
        for w_ref, wb_ref in zip(ws, wbs):
            wb_ref[...] = w_ref[...].astype(BF16)

    acc = res_ref[...]
    xi = 0
    for wb_ref, count in zip(wbs, x_counts):
        k0 = 0
        for x_ref in xs[xi:xi + count]:
            k = x_ref.shape[1]
            acc = acc + jnp.dot(x_ref[...], wb_ref[k0:k0 + k, :], preferred_element_type=F32)
            k0 += k
        xi += count
    o_ref[...] = acc


def _matmul_res(parts, layer, res, name):
    m, n = res.shape
    tm = _row_tile(m, TM_RES)
    tn = TN_RES
    assert n % tn == 0
    x_specs, w_specs, xs, ws, scratch = [], [], [], [], []
    for w, rblk, x_list in parts:
        ktot = sum(width for _, width, _ in x_list)
        for x, width, cb in x_list:
            xs.append(x)
            x_specs.append(pl.BlockSpec((tm, width), lambda j, i, cb=cb: (i, cb)))
        ws.append(w)
        w_specs.append(pl.BlockSpec((None, ktot, tn), lambda j, i, rblk=rblk: (layer, rblk, j)))
        scratch.append(pltpu.VMEM((ktot, tn), BF16))
    return pl.pallas_call(
        functools.partial(_mm_res_kernel, x_counts=tuple(len(p[2]) for p in parts)),
        grid=(n // tn, m // tm),
        in_specs=x_specs + w_specs + [pl.BlockSpec((tm, tn), lambda j, i: (i, j))],
        out_specs=pl.BlockSpec((tm, tn), lambda j, i: (i, j)),
        out_shape=jax.ShapeDtypeStruct((m, n), F32),
        scratch_shapes=scratch,
        input_output_aliases={len(xs) + len(ws): 0},
        compiler_params=_params("arbitrary", "arbitrary"),
        name=name,
    )(*xs, *ws, res)


def _glu_kernel(x_ref, wg_ref, wu_ref, o_ref, wgb_ref, wub_ref, *, row_split):
    @pl.when(pl.program_id(1) == 0)
    def _():
        wgb_ref[...] = wg_ref[...].astype(BF16)
        wub_ref[...] = wu_ref[...].astype(BF16)

    rows = x_ref.shape[0] // row_split
    for r in range(row_split):
        rs = slice(r * rows, (r + 1) * rows)
        x = x_ref[rs, :]
        g = jnp.dot(x, wgb_ref[...], preferred_element_type=F32)
        u = jnp.dot(x, wub_ref[...], preferred_element_type=F32)
        o_ref[rs, :] = (g * (1.0 / (1.0 + jnp.exp(-g))) * u).astype(o_ref.dtype)


def _glu(x, wg, wu, layer):
    m, k = x.shape
    n = wg.shape[2]
    tm = _row_tile(m, TM_ROWS)
    tn = TN_WIDE
    row_split = 2 if tm % (2 * PACK) == 0 else 1
    assert n % tn == 0
    w_spec = pl.BlockSpec((None, k, tn), lambda j, i: (layer, 0, j))
    return pl.pallas_call(
        functools.partial(_glu_kernel, row_split=row_split),
        grid=(n // tn, m // tm),
        in_specs=[pl.BlockSpec((tm, k), lambda j, i: (i, 0)), w_spec, w_spec],
        out_specs=pl.BlockSpec((tm, tn), lambda j, i: (i, j)),
        out_shape=jax.ShapeDtypeStruct((m, n), BF16),
        scratch_shapes=[pltpu.VMEM((k, tn), BF16), pltpu.VMEM((k, tn), BF16)],
        compiler_params=_params("arbitrary", "arbitrary"),
        name="ffn_glu",
    )(x, wg, wu)


def _span_attn_kernel(q_ref, k_ref, v_ref, *rest, skew, n_carried):
    o_ref, pk_ref, pv_ref, k_scr, v_scr, o_scr, l_scr, bias_scr = rest[n_carried:]
    n0 = pl.program_id(2)
    span = q_ref.shape[0]
    blk = ATTN_BLK
    units = span // blk
    scale = HD_A ** -0.5
    cur = (n0 % 2) * span
    prev = span - cur

    @pl.when(n0 == 0)
    def _():
        k_scr[pl.ds(prev, span), :] = jnp.zeros((span, HD_A), F32)
        v_scr[pl.ds(prev, span), :] = jnp.zeros((span, HD_A), F32)

    k_scr[pl.ds(cur, span), :] = k_ref[...]
    v_scr[pl.ds(cur, span), :] = v_ref[...]

    @pl.when(n0 == pl.num_programs(2) - 1)
    def _():
        pk_ref[...] = k_ref[...]
        pv_ref[...] = v_ref[...]

    qi = lax.broadcasted_iota(jnp.int32, (blk, blk), 0)
    ki = lax.broadcasted_iota(jnp.int32, (blk, blk), 1)
    bias_scr[0] = jnp.where(ki <= qi, 0.0, NEG_INF)
    bias_scr[1] = jnp.where(ki >= qi, 0.0, NEG_INF)
    bias_scr[2] = jnp.where(ki >= qi + jnp.where(n0 > 0, 0, blk), 0.0, NEG_INF)
    nt = (((1,), (1,)), ((), ()))

    def rows(start, dil):
        return pl.ds(start, blk) if dil == 1 else pl.ds(start, blk, stride=dil)

    def unit_starts(u, dil):
        r, nb = u % dil, u // dil
        start = r + dil * blk * nb
        prev_start = cur + start - dil * blk if nb > 0 else prev + r + span - dil * blk
        return start, prev_start, nb

    def scores(u, dil):
        start, prev_start, nb = unit_starts(u, dil)
        q = (q_ref[rows(start, dil), :] * (scale * LOG2E)).astype(BF16)
        kc = k_scr[rows(cur + start, dil), :].astype(BF16)
        kp = k_scr[rows(prev_start, dil), :].astype(BF16)
        sp = lax.dot_general(q, kp, nt, preferred_element_type=F32)
        sc = lax.dot_general(q, kc, nt, preferred_element_type=F32)
        sp = sp + bias_scr[1 if nb > 0 else 2]
        sc = sc + bias_scr[0]
        return sp, sc

    def finish(p, u, dil, sp, sc):
        start, prev_start, _ = unit_starts(u, dil)
        vc = v_scr[rows(cur + start, dil), :].astype(BF16)
        vp = v_scr[rows(prev_start, dil), :].astype(BF16)
        m = jnp.max(jnp.maximum(sp, sc), axis=-1, keepdims=True)
        pp = jnp.exp2(sp - m)
        pc = jnp.exp2(sc - m)
        den = jnp.sum(pp + pc, axis=-1, keepdims=True)
        o = jnp.dot(pp.astype(BF16), vp, preferred_element_type=F32)
        o = o + jnp.dot(pc.astype(BF16), vc, preferred_element_type=F32)
        o_scr[p, rows(start, dil), :] = o / den
        l_scr[p, rows(start, dil), :] = jnp.broadcast_to(m + jnp.log(den) * LOG2E, (blk, HD_A))

    order = [(p, u, dil) for p, (_, dil) in enumerate(DILATED_PATTERNS) for u in range(units)]
    pending = []
    for step in range(len(order) + skew):
        if step < len(order):
            p, u, dil = order[step]
            pending.append((p, u, dil) + scores(u, dil))
        if step >= skew:
            finish(*pending.pop(0))

    ct = 64

    def combine(c, carry):
        rs = pl.ds(pl.multiple_of(c * ct, ct), ct)
        l1, l2, l3 = l_scr[0, rs, :], l_scr[1, rs, :], l_scr[2, rs, :]
        mx = jnp.maximum(jnp.maximum(l1, l2), l3)
        e1, e2, e3 = jnp.exp2(l1 - mx), jnp.exp2(l2 - mx), jnp.exp2(l3 - mx)
        num = e1 * o_scr[0, rs, :] + e2 * o_scr[1, rs, :] + e3 * o_scr[2, rs, :]
        o_ref[rs, :] = (num / (e1 + e2 + e3)).astype(o_ref.dtype)
        return carry

    lax.fori_loop(0, span // ct, combine, 0, unroll=2)


def _span_attn(slabs, batch, seq, layer, depth, win_k, win_v):
    span = WIN_MAX
    assert seq % span == 0 and len(DILATED_PATTERNS) == 3
    nsp = seq // span
    slab = lambda s0: pl.BlockSpec((None, span, HD_A), lambda b, h, n: (s0 + h, b * nsp + n, 0))
    win_spec = pl.BlockSpec((None, None, None, span, HD_A), lambda b, h, n: (layer, b, h, 0, 0))
    win_shape = jax.ShapeDtypeStruct((depth, batch, H_A, span, HD_A), F32)
    in_specs = [slab(SLAB_Q), slab(SLAB_K), slab(SLAB_V)]
    args = [slabs, slabs, slabs]
    aliases = {}
    if win_k is not None:
        in_specs += [pl.BlockSpec(memory_space=pl.ANY)] * 2
        args += [win_k, win_v]
        aliases = {3: 1, 4: 2}
    skew = 4
    return pl.pallas_call(
        functools.partial(_span_attn_kernel, skew=skew, n_carried=len(aliases)),
        grid=(batch, H_A, nsp),
        in_specs=in_specs,
        out_specs=[pl.BlockSpec((span, HD_A), lambda b, h, n: (b * nsp + n, h)), win_spec, win_spec],
        out_shape=[jax.ShapeDtypeStruct((slabs.shape[1], D_A), BF16), win_shape, win_shape],
        scratch_shapes=[pltpu.VMEM((2 * span, HD_A), F32), pltpu.VMEM((2 * span, HD_A), F32),
                        pltpu.VMEM((3, span, HD_A), F32), pltpu.VMEM((3, span, HD_A), F32),
                        pltpu.VMEM((3, ATTN_BLK, ATTN_BLK), F32)],
        input_output_aliases=aliases,
        compiler_params=_params("arbitrary", "arbitrary", "arbitrary"),
        name="span_attn",
    )(*args)


def _dec_attn_kernel(q_ref, kn_ref, vn_ref, kc_ref, vc_ref, o_ref, *, past):
    t = q_ref.shape[1]
    scale = HD_A ** -0.5
    nt = (((1,), (1,)), ((), ()))
    d_c = past + lax.broadcasted_iota(jnp.int32, (t, past), 0) - lax.broadcasted_iota(jnp.int32, (t, past), 1)
    d_n = lax.broadcasted_iota(jnp.int32, (t, t), 0) - lax.broadcasted_iota(jnp.int32, (t, t), 1)
    masks = []
    for window, dil in DILATED_PATTERNS:
        m_c = jnp.logical_and((d_c & (dil - 1)) == 0, d_c <= window)
        m_n = jnp.logical_and(jnp.logical_and(d_n >= 0, (d_n & (dil - 1)) == 0), d_n <= window)
        masks.append((m_c, m_n))
    for h in range(H_A):
        q = q_ref[h]
        vc = vc_ref[h]
        vn = vn_ref[h]
        s_c = lax.dot_general(q, kc_ref[h], nt, preferred_element_type=F32) * scale
        s_n = lax.dot_general(q, kn_ref[h], nt, preferred_element_type=F32) * scale
        outs, lses = [], []
        for m_c, m_n in masks:
            a_c = jnp.where(m_c, s_c, NEG_INF)
            a_n = jnp.where(m_n, s_n, NEG_INF)
            m = jnp.maximum(jnp.max(a_c, axis=-1, keepdims=True), jnp.max(a_n, axis=-1, keepdims=True))
            p_c = jnp.exp(a_c - m)
            p_n = jnp.exp(a_n - m)
            den = jnp.sum(p_c, axis=-1, keepdims=True) + jnp.sum(p_n, axis=-1, keepdims=True)
            o = jnp.dot(p_c, vc, preferred_element_type=F32) + jnp.dot(p_n, vn, preferred_element_type=F32)
            outs.append(o / den)
            lses.append(m + jnp.log(den))
        mx = jnp.maximum(jnp.maximum(lses[0], lses[1]), lses[2])
        es = [jnp.exp(l - mx) for l in lses]
        num = es[0] * outs[0] + es[1] * outs[1] + es[2] * outs[2]
        o_ref[:, h * HD_A:(h + 1) * HD_A] = (num / (es[0] + es[1] + es[2])).astype(o_ref.dtype)


def _dec_attn(slabs, row0, cache_k, cache_v, layer, batch, t):
    past = cache_k.shape[3]
    assert row0 % t == 0 and SLAB_Q % H_A == 0 and SLAB_K % H_A == 0 and SLAB_V % H_A == 0
    new_spec = lambda s0: pl.BlockSpec((H_A, t, HD_A), lambda b: (s0 // H_A, row0 // t + b, 0))
    cache_spec = pl.BlockSpec((None, None, H_A, past, HD_A), lambda b: (layer, b, 0, 0, 0))
    return pl.pallas_call(
        functools.partial(_dec_attn_kernel, past=past),
        grid=(batch,),
        in_specs=[new_spec(SLAB_Q), new_spec(SLAB_K), new_spec(SLAB_V), cache_spec, cache_spec],
        out_specs=pl.BlockSpec((t, D_A), lambda b: (b, 0)),
        out_shape=jax.ShapeDtypeStruct((batch * t, D_A), F32),
        compiler_params=_params("arbitrary"),
        name="dec_attn",
    )(slabs, slabs, slabs, cache_k, cache_v)


def _pool_kernel(*refs, pos0, has_prev):
    if has_prev:
        u_ref, prev_ref, hist_ref, w_ref, sc_ref, o_ref, ext_ref = refs
    else:
        u_ref, hist_ref, w_ref, sc_ref, o_ref, ext_ref = refs
        prev_ref = None
    i = pl.program_id(1)
    tm = u_ref.shape[1]

    @pl.when(i == 0)
    def _():
        for gi in range(N_POOL):
            ext_ref[gi, 0:HIST_ROWS, :] = hist_ref[0, :, gi * POOL_GROUP:(gi + 1) * POOL_GROUP]

    if has_prev:
        @pl.when(i > 0)
        def _():
            ext_ref[:, 0:HIST_ROWS, :] = prev_ref[...]

    ext_ref[:, HIST_ROWS:HIST_ROWS + tm, :] = u_ref[...]
    pos = pos0 + i * tm + lax.broadcasted_iota(jnp.int32, (tm, POOL_GROUP), 0)
    for gi, w in enumerate(POOL_WINDOWS):
        cs = slice(gi * POOL_GROUP, (gi + 1) * POOL_GROUP)
        tok = ext_ref[gi, HIST_ROWS:HIST_ROWS + tm, :]
        tot = tok
        for back in range(1, w):
            tot = tot + ext_ref[gi, HIST_ROWS - back:HIST_ROWS - back + tm, :]
        cnt = jnp.minimum(w, pos + 1).astype(F32)
        pooled = tot / cnt - tok
        out = jnp.dot(pooled.astype(BF16), w_ref[gi].astype(BF16), preferred_element_type=F32)
        o_ref[:, cs] = (out * sc_ref[:, cs]).astype(o_ref.dtype)


def _pool(slabs, row0, out_rows, hist, w_pool, scale, layer, batch, t, pos0):
    tm = _row_tile(t, 512)
    nt = t // tm
    has_prev = nt > 1
    assert row0 % tm == 0
    rb = lambda b, i: row0 // tm + b * nt + i
    ob = lambda b, i: b * nt + i
    in_specs = [pl.BlockSpec((N_POOL, tm, LANE), lambda b, i: (SLAB_U // N_POOL, rb(b, i), 0))]
    args = [slabs]
    if has_prev:
        per = tm // HIST_ROWS
        in_specs.append(pl.BlockSpec((N_POOL, HIST_ROWS, LANE),
                                     lambda b, i: (SLAB_U // N_POOL, jnp.maximum(rb(b, i) * per - 1, 0), 0)))
        args.append(slabs)
    in_specs += [
        pl.BlockSpec((1, HIST_ROWS, D_B), lambda b, i: (b, 0, 0)),
        pl.BlockSpec((None, N_POOL, POOL_GROUP, POOL_GROUP), lambda b, i: (layer, 0, 0, 0)),
        pl.BlockSpec((None, 1, D_B), lambda b, i: (layer, 0, 0)),
    ]
    args += [hist, w_pool, scale]
    return pl.pallas_call(
        functools.partial(_pool_kernel, pos0=pos0, has_prev=has_prev),
        grid=(batch, nt),
        in_specs=in_specs,
        out_specs=pl.BlockSpec((tm, D_B), lambda b, i: (ob(b, i), 0)),
        out_shape=jax.ShapeDtypeStruct((out_rows, D_B), BF16 if tm % PACK == 0 else F32),
        scratch_shapes=[pltpu.VMEM((N_POOL, HIST_ROWS + tm, LANE), F32)],
        compiler_params=_params("arbitrary", "arbitrary"),
        name="pool_mix",
    )(*args)


def _gla_kernel(q_ref, k_ref, v_ref, r_ref, ga_ref, w2_ref, bg_ref, gn_ref, s0_ref, o_ref, s_out_ref,
                st_ref, g_ref, *, chunk, group):
    i = pl.program_id(1)
    ts = q_ref.shape[1]

    @pl.when(i == 0)
    def _():
        st_ref[...] = s0_ref[0]

    z = jnp.dot(ga_ref[...], w2_ref[...], preferred_element_type=F32) + bg_ref[...]
    g_ref[...] = (jnp.minimum(z, 0.0) - jnp.log(1.0 + jnp.exp(-jnp.abs(z)))) / GATE_NORMALIZER

    gr = group * chunk
    ri = lax.broadcasted_iota(jnp.int32, (gr, gr), 0)
    ci = lax.broadcasted_iota(jnp.int32, (gr, gr), 1)
    same_chunk = (ri // chunk) == (ci // chunk)
    tril = jnp.logical_and(same_chunk, ci <= ri)
    sum_mat = jnp.concatenate([tril.astype(F32), same_chunk.astype(F32)], axis=0)
    split_sums = gr % PACK == 0
    qscale = DK_C ** -0.5
    nt = (((1,), (1,)), ((), ()))
    tn = (((0,), (0,)), ((), ()))

    def body(gi, carry):
        rows = pl.ds(pl.multiple_of(gi * gr, gr), gr)
        if split_sums:
            rem = g_ref[rows, :]
            sums = None
            for _ in range(3):
                part = rem.astype(BF16)
                rem = rem - part.astype(F32)
                d = jnp.dot(sum_mat.astype(BF16), part, preferred_element_type=F32)
                sums = d if sums is None else sums + d
        else:
            sums = jnp.dot(sum_mat, g_ref[rows, :], preferred_element_type=F32, precision=lax.Precision.HIGHEST)
        b = sums[:gr]
        bl = sums[gr:]
        eb = jnp.exp(b)
        enb = jnp.exp(-b)
        ebl = jnp.exp(bl)
        ekd = jnp.exp(bl - b)
        heads = range(H_C)
        ksl = [slice(h * DK_P, (h + 1) * DK_P) for h in heads]
        vh = [jnp.concatenate([v_ref[VS * h + s, rows, :] for s in range(VS)], axis=-1).astype(BF16) for h in heads]
        qe = [(q_ref[h, rows, :] * qscale * eb[:, ksl[h]]).astype(BF16) for h in heads]
        ke = [(k_ref[h, rows, :] * enb[:, ksl[h]]).astype(BF16) for h in heads]
        kd = [(k_ref[h, rows, :] * ekd[:, ksl[h]]).astype(BF16) for h in heads]
        a = [lax.dot_general(qe[h], ke[h], nt, preferred_element_type=F32) for h in heads]
        st = [st_ref[h] for h in heads]
        inter = [[] for _ in heads]
        for c in range(group):
            cr = slice(c * chunk, (c + 1) * chunk)
            kv = [lax.dot_general(vh[h][cr], kd[h][cr], tn, preferred_element_type=F32) for h in heads]
            for h in heads:
                inter[h].append(lax.dot_general(qe[h][cr], st[h].astype(BF16), nt,
                                                preferred_element_type=F32))
            st = [st[h] * ebl[c * chunk:c * chunk + 1, ksl[h]] + kv[h] for h in heads]
        for h in heads:
            st_ref[h] = st[h]
            am = jnp.where(tril, a[h], 0.0).astype(BF16)
            o = jnp.dot(am, vh[h], preferred_element_type=F32)
            o = o + (inter[h][0] if group == 1 else jnp.concatenate(inter[h], axis=0))
            ms = jnp.sum(o * o, axis=-1, keepdims=True) * (1.0 / DV_C)
            on = o * lax.rsqrt(ms + EPS) * gn_ref[...]
            rh = jnp.concatenate([r_ref[VS * h + s, rows, :] for s in range(VS)], axis=-1)
            o_ref[rows, h * DV_P:(h + 1) * DV_P] = (on * (rh * (1.0 / (1.0 + jnp.exp(-rh))))).astype(o_ref.dtype)
        return carry

    lax.fori_loop(0, ts // gr, body, 0)

    @pl.when(i == pl.num_programs(1) - 1)
    def _():
        s_out_ref[0] = st_ref[...]


def _gla(slabs, row0, out_rows, w2p, bgp, gnp, s0t, layer, batch, t):
    chunk = min(GLA_CHUNK, t)
    out_dtype = BF16 if chunk % PACK == 0 else F32
    ts = _row_tile(t, 512)
    nt = t // ts
    group = GLA_GROUP if ts % (GLA_GROUP * chunk) == 0 else 1
    assert t % chunk == 0 and ts % chunk == 0 and row0 % ts == 0
    qk_w, v_w = H_C * DK_P, H_C * DV_P
    rb = lambda b, i: row0 // ts + b * nt + i
    grp = lambda n, s0: pl.BlockSpec((n, ts, LANE), lambda b, i: (s0 // n, rb(b, i), 0))
    assert SLAB_VC % (H_C * VS) == 0 and SLAB_RC % (H_C * VS) == 0 and SLAB_QC % H_C == 0 and SLAB_KC % H_C == 0
    par = lambda shape: pl.BlockSpec((None,) + shape, lambda b, i: (layer,) + (0,) * len(shape))
    state_spec = pl.BlockSpec((1, H_C, DV_P, DK_P), lambda b, i: (b, 0, 0, 0))
    return pl.pallas_call(
        functools.partial(_gla_kernel, chunk=chunk, group=group),
        grid=(batch, nt),
        in_specs=[grp(H_C, SLAB_QC), grp(H_C, SLAB_KC), grp(H_C * VS, SLAB_VC), grp(H_C * VS, SLAB_RC),
                  pl.BlockSpec((None, ts, LANE), lambda b, i: (SLAB_G, rb(b, i), 0)),
                  par((LANE, qk_w)), par((1, qk_w)), par((1, DV_P)), state_spec],
        out_specs=[pl.BlockSpec((ts, v_w), lambda b, i: (b * nt + i, 0)), state_spec],
        out_shape=[jax.ShapeDtypeStruct((out_rows, v_w), out_dtype),
                   jax.ShapeDtypeStruct((batch, H_C, DV_P, DK_P), F32)],
        scratch_shapes=[pltpu.VMEM((H_C, DV_P, DK_P), F32), pltpu.VMEM((ts, qk_w), F32)],
        compiler_params=_params("arbitrary", "arbitrary"),
        name="gla",
    )(slabs, slabs, slabs, slabs, slabs, w2p, bgp, gnp, s0t)


def _pad_head_rows(w, width, padded):
    depth, _, k = w.shape
    w = w.reshape(depth, H_C, width, k)
    w = jnp.pad(w, ((0, 0), (0, 0), (0, padded - width), (0, 0)))
    return w.reshape(depth, H_C * padded, k)


def _pad_heads(w, width, padded):
    lead = w.shape[:-1]
    w = w.reshape(lead + (H_C, width))
    w = jnp.pad(w, [(0, 0)] * len(lead) + [(0, 0), (0, padded - width)])
    return w.reshape(lead + (H_C * padded,))


def _relayout_weights(w_in, w_gate2, b_gate, gla_norm_g, w_o):
    wt = jnp.swapaxes(w_in, 1, 2)
    c0 = 3 * D_A + D_B
    qk = H_C * DK_C
    wq = wt[:, c0:c0 + qk]
    wk = wt[:, c0 + qk:c0 + 2 * qk]
    wv = wt[:, c0 + 2 * qk:c0 + 2 * qk + D_C]
    wr = wt[:, c0 + 2 * qk + D_C:c0 + 2 * qk + 2 * D_C]
    wg = wt[:, c0 + 2 * qk + 2 * D_C:]
    w_all = jnp.concatenate([
        _pad_head_rows(wv, DV_C, DV_P), _pad_head_rows(wr, DV_C, DV_P),
        _pad_head_rows(wq, DK_C, DK_P), _pad_head_rows(wk, DK_C, DK_P),
        wt[:, 3 * D_A:c0],
        jnp.pad(wg, ((0, 0), (0, 2 * LANE - GATE_RANK), (0, 0))),
        wt[:, :3 * D_A],
    ], axis=1).astype(BF16)
    assert w_all.shape[1] == N_SLABS * LANE
    w2p = jnp.pad(_pad_heads(w_gate2, DK_C, DK_P), ((0, 0), (0, LANE - GATE_RANK), (0, 0)))
    bgp = _pad_heads(b_gate, DK_C, DK_P)[:, None, :]
    gnp = jnp.pad(gla_norm_g, ((0, 0), (0, DV_P - DV_C)))[:, None, :]
    wo_c = _pad_head_rows(w_o[:, D_A + D_B:], DV_C, DV_P)
    return w_all, w2p, bgp, gnp, wo_c


def _state_to_kernel(s):
    s = jnp.swapaxes(s, 2, 3)
    return jnp.pad(s, ((0, 0), (0, 0), (0, DV_P - DV_C), (0, DK_P - DK_C)))


def _state_from_kernel(s):
    return jnp.swapaxes(s[:, :, :DV_C, :DK_C], 2, 3)


def _heads_to_rows(a):
    return jnp.transpose(a, (0, 1, 3, 2, 4))


def _set_rows(big, row0, small):
    return lax.dynamic_update_slice(big, small.astype(big.dtype), (row0, 0))


def kernel(x_prompt, x_sample, cache_win_k, cache_win_v, state_pool, state_gla, norm1_g, w_in, w_gate2, b_gate,
           w_pool, pool_scale, gla_norm_g, w_o, norm2_g, w_ffn_gate, w_ffn_up, w_ffn_down, final_norm_g):
    bp, sp, d_model = x_prompt.shape
    bs, ts, _ = x_sample.shape
    depth = w_in.shape[0]
    d_ff = w_ffn_down.shape[1]
    mp, ms = bp * sp, bs * ts
    m_all = mp + ms
    assert min(WIN_MAX, sp) == WIN_MAX
    assert d_ff % 2 == 0

    w_all, w2p, bgp, gnp, wo_c = _relayout_weights(w_in, w_gate2, b_gate, gla_norm_g, w_o)
    cache_k = jnp.transpose(cache_win_k, (0, 1, 3, 2, 4))
    cache_v = jnp.transpose(cache_win_v, (0, 1, 3, 2, 4))
    pool_scale3 = pool_scale[:, None, :]

    x = jnp.concatenate([x_prompt.reshape(mp, d_model), x_sample.reshape(ms, d_model)], axis=0)
    zero_hist = jnp.zeros((bp, HIST_ROWS, D_B), F32)
    zero_state = jnp.zeros((bp, H_C, DV_P, DK_P), F32)
    p_pool, p_gla, s_k, s_v, s_pool, s_gla = [], [], [], [], [], []
    win_k = win_v = None
    for l in range(depth):
        slabs = _proj(x, norm1_g[l], w_all, l)

        out_a, win_k, win_v = _span_attn(slabs, bp, sp, l, depth, win_k, win_v)
        out_b = _pool(slabs, 0, m_all, zero_hist, w_pool, pool_scale3, l, bp, sp, 0)
        out_c, st_p = _gla(slabs, 0, m_all, w2p, bgp, gnp, zero_state, l, bp, sp)
        u_p = jnp.stack([slabs[SLAB_U:SLAB_U + N_POOL, (b + 1) * sp - POOL_BUF:(b + 1) * sp] for b in range(bp)])
        p_pool.append(jnp.transpose(u_p, (0, 2, 1, 3)).reshape(bp, POOL_BUF, D_B))
        p_gla.append(_state_from_kernel(st_p))

        hist = jnp.pad(state_pool[l], ((0, 0), (HIST_ROWS - POOL_BUF, 0), (0, 0)))
        sa = _dec_attn(slabs, mp, cache_k, cache_v, l, bs, ts)
        sb = _pool(slabs, mp, ms, hist, w_pool, pool_scale3, l, bs, ts, PAST_LEN)
        sc, st_s = _gla(slabs, mp, ms, w2p, bgp, gnp, _state_to_kernel(state_gla[l]), l, bs, ts)
        out_a = _set_rows(out_a, mp, sa)
        out_b = _set_rows(out_b, mp, sb)
        out_c = _set_rows(out_c, mp, sc)
        new_kv = slabs[SLAB_K:SLAB_K + 2 * H_A, mp:].reshape(2, H_A, bs, ts, HD_A)
        s_k.append(jnp.swapaxes(new_kv[0], 0, 1))
        s_v.append(jnp.swapaxes(new_kv[1], 0, 1))
        u_s = jnp.transpose(slabs[SLAB_U:SLAB_U + N_POOL, mp:].reshape(N_POOL, bs, ts, LANE), (1, 2, 0, 3))
        u_ext = jnp.concatenate([state_pool[l], u_s.reshape(bs, ts, D_B)], axis=1)
        s_pool.append(u_ext[:, -POOL_BUF:])
        s_gla.append(_state_from_kernel(st_s))

        x = _matmul_res([(w_o, 0, [(out_a, D_A, 0), (out_b, D_B, 0)]), (wo_c, 0, [(out_c, H_C * DV_P, 0)])],
                        l, x, "out_proj")
        h2 = _rmsnorm(x, norm2_g[l], BF16)
        f = _glu(h2, w_ffn_gate, w_ffn_up, l)
        half = d_ff // 2
        x = _matmul_res([(w_ffn_down, 0, [(f, half, 0)])], l, x, "ffn_down")
        x = _matmul_res([(w_ffn_down, 1, [(f, half, 1)])], l, x, "ffn_down")

    y_prompt = _rmsnorm(x, final_norm_g, F32, row0=0, rows=mp).reshape(bp, sp, d_model)
    y_sample = _rmsnorm(x, final_norm_g, F32, row0=mp, rows=ms).reshape(bs, ts, d_model)
    return (y_prompt, y_sample, _heads_to_rows(win_k), _heads_to_rows(win_v),
            jnp.stack(p_pool), jnp.stack(p_gla), _heads_to_rows(jnp.stack(s_k)), _heads_to_rows(jnp.stack(s_v)),
            jnp.stack(s_pool), jnp.stack(s_gla))
```

```python
import functools

import jax
import jax.numpy as jnp
from jax import lax
from jax.experimental import pallas as pl
from jax.experimental.pallas import tpu as pltpu

F32 = jnp.float32
BF16 = jnp.bfloat16

H_A, HD_A = 6, 128
D_A = H_A * HD_A
DILATED_PATTERNS = ((128, 1), (512, 4), (2048, 16))
WIN_MAX = 2048
POOL_WINDOWS = (2, 4, 8, 16)
N_POOL = 4
D_B = 512
POOL_GROUP = D_B // N_POOL
POOL_BUF = 15
H_C, DK_C, DV_C = 4, 96, 192
D_C = H_C * DV_C
GATE_RANK = 16
GATE_NORMALIZER = 16.0
GLA_CHUNK = 32
PAST_LEN = 16384
EPS = 1e-6
NEG_INF = -1e30
LOG2E = 1.4426950408889634

LANE = 128
SUBLANE = 8
PACK = 16
VMEM_LIMIT_BYTES = 56 * 1024 * 1024

DK_P = 128
DV_P = 256
HIST_ROWS = 16

VS = DV_P // LANE
SLAB_VC = 0
SLAB_RC = SLAB_VC + H_C * VS
SLAB_QC = SLAB_RC + H_C * VS
SLAB_KC = SLAB_QC + H_C
SLAB_U = SLAB_KC + H_C
SLAB_G = SLAB_U + N_POOL
SLAB_Q = SLAB_G + 2
SLAB_K = SLAB_Q + H_A
SLAB_V = SLAB_K + H_A
N_SLABS = SLAB_V + H_A
D_MIX_P = D_A + D_B + H_C * DV_P

ATTN_BLK = 128
assert all(w // d == ATTN_BLK and d & (d - 1) == 0 for w, d in DILATED_PATTERNS)

TM_ROWS = 1376
TM_NORM = 688
TN_PROJ = 768
TN_WIDE = 512
TM_RES = 688
TN_RES = 1024
GLA_GROUP = 8


def _params(*sem):
    return pltpu.CompilerParams(dimension_semantics=sem, vmem_limit_bytes=VMEM_LIMIT_BYTES)


def _row_tile(m, target, mult=PACK):
    if m <= target:
        return m
    t = target - target % mult
    while t >= mult:
        if m % t == 0:
            return t
        t -= mult
    raise ValueError(f"no row tile for {m}")


def _rmsnorm_kernel(x_ref, g_ref, o_ref):
    x = x_ref[...]
    ms = jnp.mean(x * x, axis=-1, keepdims=True)
    o_ref[...] = (x * lax.rsqrt(ms + EPS) * g_ref[...]).astype(o_ref.dtype)


def _rmsnorm(x, g, out_dtype, *, row0=0, rows=None):
    d = x.shape[1]
    rows = x.shape[0] if rows is None else rows
    tm = _row_tile(rows, TM_NORM)
    assert row0 % tm == 0
    return pl.pallas_call(
        _rmsnorm_kernel,
        grid=(rows // tm,),
        in_specs=[pl.BlockSpec((tm, d), lambda i: (i + row0 // tm, 0)), pl.BlockSpec((1, d), lambda i: (0, 0))],
        out_specs=pl.BlockSpec((tm, d), lambda i: (i, 0)),
        out_shape=jax.ShapeDtypeStruct((rows, d), out_dtype),
        compiler_params=_params("arbitrary"),
        name="rmsnorm",
    )(x, g.reshape(1, d))


def _proj_kernel(x_ref, g_ref, w_ref, o_ref, h_ref):
    @pl.when(pl.program_id(1) == 0)
    def _():
        x = x_ref[...]
        ms = jnp.mean(x * x, axis=-1, keepdims=True)
        h_ref[...] = (x * lax.rsqrt(ms + EPS) * g_ref[...]).astype(BF16)

    acc = lax.dot_general(h_ref[...], w_ref[...], (((1,), (1,)), ((), ())), preferred_element_type=F32)
    for c in range(o_ref.shape[0]):
        o_ref[c] = acc[:, c * LANE:(c + 1) * LANE]


def _proj(x, g, w_all, layer):
    m, k = x.shape
    n = w_all.shape[1]
    tm = _row_tile(m, TM_ROWS)
    tn = TN_PROJ
    assert n % tn == 0
    return pl.pallas_call(
        _proj_kernel,
        grid=(m // tm, n // tn),
        in_specs=[pl.BlockSpec((tm, k), lambda i, j: (i, 0)),
                  pl.BlockSpec((1, k), lambda i, j: (0, 0)),
                  pl.BlockSpec((None, tn, k), lambda i, j: (layer, j, 0))],
        out_specs=pl.BlockSpec((tn // LANE, tm, LANE), lambda i, j: (j, i, 0)),
        out_shape=jax.ShapeDtypeStruct((n // LANE, m, LANE), F32),
        scratch_shapes=[pltpu.VMEM((tm, k), BF16)],
        compiler_params=_params("arbitrary", "arbitrary"),
        name="proj_in",
    )(x, g.reshape(1, k), w_all)


def _mm_res_kernel(*refs, x_counts):
    n_x, n_w = sum(x_counts), len(x_counts)
    xs = refs[:n_x]
    ws = refs[n_x:n_x + n_w]
    res_ref, o_ref = refs[n_x + n_w:n_x + n_w + 2]
    wbs = refs[n_x + n_w + 2:]

    @pl.when(pl.program_id(1) == 0)
    def _():
        for w_ref, wb_ref in zip(ws, wbs):
            wb_ref[...] = w_ref[...].astype(BF16)

    acc = res_ref[...]
    xi = 0
    for wb_ref, count in zip(wbs, x_counts):
        k0 = 0
        for x_ref in xs[xi:xi + count]:
            k = x_ref.shape[1]
            acc = acc + jnp.dot(x_ref[...], wb_ref[k0:k0 + k, :], preferred_element_type=F32)
            k0 += k
        xi += count
    o_ref[...] = acc


def _matmul_res(parts, layer, res, name):
    m, n = res.shape
    tm = _row_tile(m, TM_RES)
    tn = TN_RES
    assert n % tn == 0
    x_specs, w_specs, xs, ws, scratch = [], [], [], [], []
    for w, rblk, x_list in parts:
        ktot = sum(width for _, width, _ in x_list)
        for x, width, cb in x_list:
            xs.append(x)
            x_specs.append(pl.BlockSpec((tm, width), lambda j, i, cb=cb: (i, cb)))
        ws.append(w)
        w_specs.append(pl.BlockSpec((None, ktot, tn), lambda j, i, rblk=rblk: (layer, rblk, j)))
        scratch.append(pltpu.VMEM((ktot, tn), BF16))
    return pl.pallas_call(
        functools.partial(_mm_res_kernel, x_counts=tuple(len(p[2]) for p in parts)),
        grid=(n // tn, m // tm),
        in_specs=x_specs + w_specs + [pl.BlockSpec((tm, tn), lambda j, i: (i, j))],
        out_specs=pl.BlockSpec((tm, tn), lambda j, i: (i, j)),
        out_shape=jax.ShapeDtypeStruct((m, n), F32),
        scratch_shapes=scratch,
        input_output_aliases={len(xs) + len(ws): 0},
        compiler_params=_params("arbitrary", "arbitrary"),
        name=name,
    )(*xs, *ws, res)


def _glu_kernel(x_ref, wg_ref, wu_ref, o_ref, wgb_ref, wub_ref, *, row_split):
    @pl.when(pl.program_id(1) == 0)
    def _():
        wgb_ref[...] = wg_ref[...].astype(BF16)
        wub_ref[...] = wu_ref[...].astype(BF16)

    rows = x_ref.shape[0] // row_split
    for r in range(row_split):
        rs = slice(r * rows, (r + 1) * rows)
        x = x_ref[rs, :]
        g = jnp.dot(x, wgb_ref[...], preferred_element_type=F32)
        u = jnp.dot(x, wub_ref[...], preferred_element_type=F32)
        o_ref[rs, :] = (g * (1.0 / (1.0 + jnp.exp(-g))) * u).astype(o_ref.dtype)


def _glu(x, wg, wu, layer):
    m, k = x.shape
    n = wg.shape[2]
    tm = _row_tile(m, TM_ROWS)
    tn = TN_WIDE
    row_split = 2 if tm % (2 * PACK) == 0 else 1
    assert n % tn == 0
    w_spec = pl.BlockSpec((None, k, tn), lambda j, i: (layer, 0, j))
    return pl.pallas_call(
        functools.partial(_glu_kernel, row_split=row_split),
        grid=(n // tn, m // tm),
        in_specs=[pl.BlockSpec((tm, k), lambda j, i: (i, 0)), w_spec, w_spec],
        out_specs=pl.BlockSpec((tm, tn), lambda j, i: (i, j)),
        out_shape=jax.ShapeDtypeStruct((m, n), BF16),
        scratch_shapes=[pltpu.VMEM((k, tn), BF16), pltpu.VMEM((k, tn), BF16)],
        compiler_params=_params("arbitrary", "arbitrary"),
        name="ffn_glu",
    )(x, wg, wu)


def _span_attn_kernel(q_ref, k_ref, v_ref, *rest, skew, n_carried):
    o_ref, pk_ref, pv_ref, k_scr, v_scr, o_scr, l_scr, bias_scr = rest[n_carried:]
    n0 = pl.program_id(2)
    span = q_ref.shape[0]
    blk = ATTN_BLK
    units = span // blk
    scale = HD_A ** -0.5
    cur = (n0 % 2) * span
    prev = span - cur

    k_scr[pl.ds(cur, span), :] = k_ref[...]
    v_scr[pl.ds(cur, span), :] = v_ref[...]

    @pl.when(n0 == pl.num_programs(2) - 1)
    def _():
        pk_ref[...] = k_ref[...]
        pv_ref[...] = v_ref[...]

    qi = lax.broadcasted_iota(jnp.int32, (blk, blk), 0)
    ki = lax.broadcasted_iota(jnp.int32, (blk, blk), 1)
    bias_scr[0] = jnp.where(ki <= qi, 0.0, NEG_INF)
    bias_scr[1] = jnp.where(ki >= qi, 0.0, NEG_INF)
    nt = (((1,), (1,)), ((), ()))

    def rows(start, dil):
        return pl.ds(start, blk) if dil == 1 else pl.ds(start, blk, stride=dil)

    def unit_starts(u, dil):
        r, nb = u % dil, u // dil
        start = r + dil * blk * nb
        prev_start = cur + start - dil * blk if nb > 0 else prev + r + span - dil * blk
        return start, prev_start, nb

    def run_units(first_span):
        def scores(u, dil):
            start, prev_start, nb = unit_starts(u, dil)
            q = (q_ref[rows(start, dil), :] * (scale * LOG2E)).astype(BF16)
            kc = k_scr[rows(cur + start, dil), :].astype(BF16)
            sc = lax.dot_general(q, kc, nt, preferred_element_type=F32) + bias_scr[0]
            if first_span and nb == 0:
                return None, sc
            kp = k_scr[rows(prev_start, dil), :].astype(BF16)
            sp = lax.dot_general(q, kp, nt, preferred_element_type=F32) + bias_scr[1]
            return sp, sc

        def finish(p, u, dil, sp, sc):
            start, prev_start, _ = unit_starts(u, dil)
            vc = v_scr[rows(cur + start, dil), :].astype(BF16)
            if sp is None:
                m = jnp.max(sc, axis=-1, keepdims=True)
                pc = jnp.exp2(sc - m)
                den = jnp.sum(pc, axis=-1, keepdims=True)
                o = jnp.dot(pc.astype(BF16), vc, preferred_element_type=F32)
            else:
                vp = v_scr[rows(prev_start, dil), :].astype(BF16)
                m = jnp.max(jnp.maximum(sp, sc), axis=-1, keepdims=True)
                pp = jnp.exp2(sp - m)
                pc = jnp.exp2(sc - m)
                den = jnp.sum(pp + pc, axis=-1, keepdims=True)
                o = jnp.dot(pp.astype(BF16), vp, preferred_element_type=F32)
                o = o + jnp.dot(pc.astype(BF16), vc, preferred_element_type=F32)
            o_scr[p, rows(start, dil), :] = o / den
            l_scr[p, rows(start, dil), :] = jnp.broadcast_to(m + jnp.log(den) * LOG2E, (blk, HD_A))

        order = [(p, u, dil) for p, (_, dil) in enumerate(DILATED_PATTERNS) for u in range(units)]
        pending = []
        for step in range(len(order) + skew):
            if step < len(order):
                p, u, dil = order[step]
                pending.append((p, u, dil) + scores(u, dil))
            if step >= skew:
                finish(*pending.pop(0))

    @pl.when(n0 == 0)
    def _():
        run_units(True)

    @pl.when(n0 > 0)
    def _():
        run_units(False)

    ct = 64

    def combine(c, carry):
        rs = pl.ds(pl.multiple_of(c * ct, ct), ct)
        l1, l2, l3 = l_scr[0, rs, :], l_scr[1, rs, :], l_scr[2, rs, :]
        mx = jnp.maximum(jnp.maximum(l1, l2), l3)
        e1, e2, e3 = jnp.exp2(l1 - mx), jnp.exp2(l2 - mx), jnp.exp2(l3 - mx)
        num = e1 * o_scr[0, rs, :] + e2 * o_scr[1, rs, :] + e3 * o_scr[2, rs, :]
        o_ref[rs, :] = (num / (e1 + e2 + e3)).astype(o_ref.dtype)
        return carry

    lax.fori_loop(0, span // ct, combine, 0, unroll=2)


def _span_attn(slabs, batch, seq, layer, depth, win_k, win_v):
    span = WIN_MAX
    assert seq % span == 0 and len(DILATED_PATTERNS) == 3
    nsp = seq // span
    slab = lambda s0: pl.BlockSpec((None, span, HD_A), lambda b, h, n: (s0 + h, b * nsp + n, 0))
    win_spec = pl.BlockSpec((None, None, None, span, HD_A), lambda b, h, n: (layer, b, h, 0, 0))
    win_shape = jax.ShapeDtypeStruct((depth, batch, H_A, span, HD_A), F32)
    in_specs = [slab(SLAB_Q), slab(SLAB_K), slab(SLAB_V)]
    args = [slabs, slabs, slabs]
    aliases = {}
    if win_k is not None:
        in_specs += [pl.BlockSpec(memory_space=pl.ANY)] * 2
        args += [win_k, win_v]
        aliases = {3: 1, 4: 2}
    skew = 4
    return pl.pallas_call(
        functools.partial(_span_attn_kernel, skew=skew, n_carried=len(aliases)),
        grid=(batch, H_A, nsp),
        in_specs=in_specs,
        out_specs=[pl.BlockSpec((span, HD_A), lambda b, h, n: (b * nsp + n, h)), win_spec, win_spec],
        out_shape=[jax.ShapeDtypeStruct((slabs.shape[1], D_A), BF16), win_shape, win_shape],
        scratch_shapes=[pltpu.VMEM((2 * span, HD_A), F32), pltpu.VMEM((2 * span, HD_A), F32),
                        pltpu.VMEM((3, span, HD_A), F32), pltpu.VMEM((3, span, HD_A), F32),
                        pltpu.VMEM((2, ATTN_BLK, ATTN_BLK), F32)],
        input_output_aliases=aliases,
        compiler_params=_params("arbitrary", "arbitrary", "arbitrary"),
        name="span_attn",
    )(*args)


def _dec_attn_kernel(q_ref, kn_ref, vn_ref, kc_ref, vc_ref, o_ref, *, past):
    t = q_ref.shape[1]
    scale = HD_A ** -0.5
    nt = (((1,), (1,)), ((), ()))
    d_c = past + lax.broadcasted_iota(jnp.int32, (t, past), 0) - lax.broadcasted_iota(jnp.int32, (t, past), 1)
    d_n = lax.broadcasted_iota(jnp.int32, (t, t), 0) - lax.broadcasted_iota(jnp.int32, (t, t), 1)
    masks = []
    for window, dil in DILATED_PATTERNS:
        m_c = jnp.logical_and((d_c & (dil - 1)) == 0, d_c <= window)
        m_n = jnp.logical_and(jnp.logical_and(d_n >= 0, (d_n & (dil - 1)) == 0), d_n <= window)
        masks.append((m_c, m_n))
    for h in range(H_A):
        q = q_ref[h]
        vc = vc_ref[h]
        vn = vn_ref[h]
        s_c = lax.dot_general(q, kc_ref[h], nt, preferred_element_type=F32) * scale
        s_n = lax.dot_general(q, kn_ref[h], nt, preferred_element_type=F32) * scale
        outs, lses = [], []
        for m_c, m_n in masks:
            a_c = jnp.where(m_c, s_c, NEG_INF)
            a_n = jnp.where(m_n, s_n, NEG_INF)
            m = jnp.maximum(jnp.max(a_c, axis=-1, keepdims=True), jnp.max(a_n, axis=-1, keepdims=True))
            p_c = jnp.exp(a_c - m)
            p_n = jnp.exp(a_n - m)
            den = jnp.sum(p_c, axis=-1, keepdims=True) + jnp.sum(p_n, axis=-1, keepdims=True)
            o = jnp.dot(p_c, vc, preferred_element_type=F32) + jnp.dot(p_n, vn, preferred_element_type=F32)
            outs.append(o / den)
            lses.append(m + jnp.log(den))
        mx = jnp.maximum(jnp.maximum(lses[0], lses[1]), lses[2])
        es = [jnp.exp(l - mx) for l in lses]
        num = es[0] * outs[0] + es[1] * outs[1] + es[2] * outs[2]
        o_ref[:, h * HD_A:(h + 1) * HD_A] = (num / (es[0] + es[1] + es[2])).astype(o_ref.dtype)


def _dec_attn(slabs, row0, cache_k, cache_v, layer, batch, t):
    past = cache_k.shape[3]
    assert row0 % t == 0 and SLAB_Q % H_A == 0 and SLAB_K % H_A == 0 and SLAB_V % H_A == 0
    new_spec = lambda s0: pl.BlockSpec((H_A, t, HD_A), lambda b: (s0 // H_A, row0 // t + b, 0))
    cache_spec = pl.BlockSpec((None, None, H_A, past, HD_A), lambda b: (layer, b, 0, 0, 0))
    return pl.pallas_call(
        functools.partial(_dec_attn_kernel, past=past),
        grid=(batch,),
        in_specs=[new_spec(SLAB_Q), new_spec(SLAB_K), new_spec(SLAB_V), cache_spec, cache_spec],
        out_specs=pl.BlockSpec((t, D_A), lambda b: (b, 0)),
        out_shape=jax.ShapeDtypeStruct((batch * t, D_A), F32),
        compiler_params=_params("arbitrary"),
        name="dec_attn",
    )(slabs, slabs, slabs, cache_k, cache_v)


def _pool_kernel(*refs, pos0, has_prev):
    if has_prev:
        u_ref, prev_ref, hist_ref, w_ref, sc_ref, o_ref, ext_ref = refs
    else:
        u_ref, hist_ref, w_ref, sc_ref, o_ref, ext_ref = refs
        prev_ref = None
    i = pl.program_id(1)
    tm = u_ref.shape[1]

    @pl.when(i == 0)
    def _():
        for gi in range(N_POOL):
            ext_ref[gi, 0:HIST_ROWS, :] = hist_ref[0, :, gi * POOL_GROUP:(gi + 1) * POOL_GROUP]

    if has_prev:
        @pl.when(i > 0)
        def _():
            ext_ref[:, 0:HIST_ROWS, :] = prev_ref[...]

    ext_ref[:, HIST_ROWS:HIST_ROWS + tm, :] = u_ref[...]
    pos = pos0 + i * tm + lax.broadcasted_iota(jnp.int32, (tm, POOL_GROUP), 0)
    for gi, w in enumerate(POOL_WINDOWS):
        cs = slice(gi * POOL_GROUP, (gi + 1) * POOL_GROUP)
        tok = ext_ref[gi, HIST_ROWS:HIST_ROWS + tm, :]
        tot = tok
        for back in range(1, w):
            tot = tot + ext_ref[gi, HIST_ROWS - back:HIST_ROWS - back + tm, :]
        cnt = jnp.minimum(w, pos + 1).astype(F32)
        pooled = tot / cnt - tok
        out = jnp.dot(pooled.astype(BF16), w_ref[gi].astype(BF16), preferred_element_type=F32)
        o_ref[:, cs] = (out * sc_ref[:, cs]).astype(o_ref.dtype)


def _pool(slabs, row0, out_rows, hist, w_pool, scale, layer, batch, t, pos0):
    tm = _row_tile(t, 512)
    nt = t // tm
    has_prev = nt > 1
    assert row0 % tm == 0
    rb = lambda b, i: row0 // tm + b * nt + i
    ob = lambda b, i: b * nt + i
    in_specs = [pl.BlockSpec((N_POOL, tm, LANE), lambda b, i: (SLAB_U // N_POOL, rb(b, i), 0))]
    args = [slabs]
    if has_prev:
        per = tm // HIST_ROWS
        in_specs.append(pl.BlockSpec((N_POOL, HIST_ROWS, LANE),
                                     lambda b, i: (SLAB_U // N_POOL, jnp.maximum(rb(b, i) * per - 1, 0), 0)))
        args.append(slabs)
    in_specs += [
        pl.BlockSpec((1, HIST_ROWS, D_B), lambda b, i: (b, 0, 0)),
        pl.BlockSpec((None, N_POOL, POOL_GROUP, POOL_GROUP), lambda b, i: (layer, 0, 0, 0)),
        pl.BlockSpec((None, 1, D_B), lambda b, i: (layer, 0, 0)),
    ]
    args += [hist, w_pool, scale]
    return pl.pallas_call(
        functools.partial(_pool_kernel, pos0=pos0, has_prev=has_prev),
        grid=(batch, nt),
        in_specs=in_specs,
        out_specs=pl.BlockSpec((tm, D_B), lambda b, i: (ob(b, i), 0)),
        out_shape=jax.ShapeDtypeStruct((out_rows, D_B), BF16 if tm % PACK == 0 else F32),
        scratch_shapes=[pltpu.VMEM((N_POOL, HIST_ROWS + tm, LANE), F32)],
        compiler_params=_params("arbitrary", "arbitrary"),
        name="pool_mix",
    )(*args)


def _gla_kernel(q_ref, k_ref, v_ref, r_ref, ga_ref, w2_ref, bg_ref, gn_ref, s0_ref, o_ref, s_out_ref,
                st_ref, g_ref, *, chunk, group):
    i = pl.program_id(1)
    ts = q_ref.shape[1]

    @pl.when(i == 0)
    def _():
        st_ref[...] = s0_ref[0]

    z = jnp.dot(ga_ref[...], w2_ref[...], preferred_element_type=F32) + bg_ref[...]
    g_ref[...] = (jnp.minimum(z, 0.0) - jnp.log(1.0 + jnp.exp(-jnp.abs(z)))) / GATE_NORMALIZER

    gr = group * chunk
    ri = lax.broadcasted_iota(jnp.int32, (gr, gr), 0)
    ci = lax.broadcasted_iota(jnp.int32, (gr, gr), 1)
    same_chunk = (ri // chunk) == (ci // chunk)
    tril = jnp.logical_and(same_chunk, ci <= ri)
    sum_mat = jnp.concatenate([tril.astype(F32), same_chunk.astype(F32)], axis=0)
    split_sums = gr % PACK == 0
    qscale = DK_C ** -0.5
    nt = (((1,), (1,)), ((), ()))
    tn = (((0,), (0,)), ((), ()))

    def body(gi, carry):
        rows = pl.ds(pl.multiple_of(gi * gr, gr), gr)
        if split_sums:
            rem = g_ref[rows, :]
            sums = None
            for _ in range(3):
                part = rem.astype(BF16)
                rem = rem - part.astype(F32)
                d = jnp.dot(sum_mat.astype(BF16), part, preferred_element_type=F32)
                sums = d if sums is None else sums + d
        else:
            sums = jnp.dot(sum_mat, g_ref[rows, :], preferred_element_type=F32, precision=lax.Precision.HIGHEST)
        b = sums[:gr]
        bl = sums[gr:]
        eb = jnp.exp(b)
        enb = jnp.exp(-b)
        ebl = jnp.exp(bl)
        ekd = jnp.exp(bl - b)
        heads = range(H_C)
        ksl = [slice(h * DK_P, (h + 1) * DK_P) for h in heads]
        vh = [jnp.concatenate([v_ref[VS * h + s, rows, :] for s in range(VS)], axis=-1).astype(BF16) for h in heads]
        qe = [(q_ref[h, rows, :] * qscale * eb[:, ksl[h]]).astype(BF16) for h in heads]
        ke = [(k_ref[h, rows, :] * enb[:, ksl[h]]).astype(BF16) for h in heads]
        kd = [(k_ref[h, rows, :] * ekd[:, ksl[h]]).astype(BF16) for h in heads]
        a = [lax.dot_general(qe[h], ke[h], nt, preferred_element_type=F32) for h in heads]
        st = [st_ref[h] for h in heads]
        inter = [[] for _ in heads]
        for c in range(group):
            cr = slice(c * chunk, (c + 1) * chunk)
            kv = [lax.dot_general(vh[h][cr], kd[h][cr], tn, preferred_element_type=F32) for h in heads]
            for h in heads:
                inter[h].append(lax.dot_general(qe[h][cr], st[h].astype(BF16), nt,
                                                preferred_element_type=F32))
            st = [st[h] * ebl[c * chunk:c * chunk + 1, ksl[h]] + kv[h] for h in heads]
        for h in heads:
            st_ref[h] = st[h]
            am = jnp.where(tril, a[h], 0.0).astype(BF16)
            o = jnp.dot(am, vh[h], preferred_element_type=F32)
            o = o + (inter[h][0] if group == 1 else jnp.concatenate(inter[h], axis=0))
            ms = jnp.sum(o * o, axis=-1, keepdims=True) * (1.0 / DV_C)
            on = o * lax.rsqrt(ms + EPS) * gn_ref[...]
            rh = jnp.concatenate([r_ref[VS * h + s, rows, :] for s in range(VS)], axis=-1)
            o_ref[rows, h * DV_P:(h + 1) * DV_P] = (on * (rh * (1.0 / (1.0 + jnp.exp(-rh))))).astype(o_ref.dtype)
        return carry

    lax.fori_loop(0, ts // gr, body, 0)

    @pl.when(i == pl.num_programs(1) - 1)
    def _():
        s_out_ref[0] = st_ref[...]


def _gla(slabs, row0, out_rows, w2p, bgp, gnp, s0t, layer, batch, t):
    chunk = min(GLA_CHUNK, t)
    out_dtype = BF16 if chunk % PACK == 0 else F32
    ts = _row_tile(t, 512)
    nt = t // ts
    group = GLA_GROUP if ts % (GLA_GROUP * chunk) == 0 else 1
    assert t % chunk == 0 and ts % chunk == 0 and row0 % ts == 0
    qk_w, v_w = H_C * DK_P, H_C * DV_P
    rb = lambda b, i: row0 // ts + b * nt + i
    grp = lambda n, s0: pl.BlockSpec((n, ts, LANE), lambda b, i: (s0 // n, rb(b, i), 0))
    assert SLAB_VC % (H_C * VS) == 0 and SLAB_RC % (H_C * VS) == 0 and SLAB_QC % H_C == 0 and SLAB_KC % H_C == 0
    par = lambda shape: pl.BlockSpec((None,) + shape, lambda b, i: (layer,) + (0,) * len(shape))
    state_spec = pl.BlockSpec((1, H_C, DV_P, DK_P), lambda b, i: (b, 0, 0, 0))
    return pl.pallas_call(
        functools.partial(_gla_kernel, chunk=chunk, group=group),
        grid=(batch, nt),
        in_specs=[grp(H_C, SLAB_QC), grp(H_C, SLAB_KC), grp(H_C * VS, SLAB_VC), grp(H_C * VS, SLAB_RC),
                  pl.BlockSpec((None, ts, LANE), lambda b, i: (SLAB_G, rb(b, i), 0)),
                  par((LANE, qk_w)), par((1, qk_w)), par((1, DV_P)), state_spec],
        out_specs=[pl.BlockSpec((ts, v_w), lambda b, i: (b * nt + i, 0)), state_spec],
        out_shape=[jax.ShapeDtypeStruct((out_rows, v_w), out_dtype),
                   jax.ShapeDtypeStruct((batch, H_C, DV_P, DK_P), F32)],
        scratch_shapes=[pltpu.VMEM((H_C, DV_P, DK_P), F32), pltpu.VMEM((ts, qk_w), F32)],
        compiler_params=_params("arbitrary", "arbitrary"),
        name="gla",
    )(slabs, slabs, slabs, slabs, slabs, w2p, bgp, gnp, s0t)


def _pad_head_rows(w, width, padded):
    depth, _, k = w.shape
    w = w.reshape(depth, H_C, width, k)
    w = jnp.pad(w, ((0, 0), (0, 0), (0, padded - width), (0, 0)))
    return w.reshape(depth, H_C * padded, k)


def _pad_heads(w, width, padded):
    lead = w.shape[:-1]
    w = w.reshape(lead + (H_C, width))
    w = jnp.pad(w, [(0, 0)] * len(lead) + [(0, 0), (0, padded - width)])
    return w.reshape(lead + (H_C * padded,))


def _relayout_weights(w_in, w_gate2, b_gate, gla_norm_g, w_o):
    wt = jnp.swapaxes(w_in, 1, 2)
    c0 = 3 * D_A + D_B
    qk = H_C * DK_C
    wq = wt[:, c0:c0 + qk]
    wk = wt[:, c0 + qk:c0 + 2 * qk]
    wv = wt[:, c0 + 2 * qk:c0 + 2 * qk + D_C]
    wr = wt[:, c0 + 2 * qk + D_C:c0 + 2 * qk + 2 * D_C]
    wg = wt[:, c0 + 2 * qk + 2 * D_C:]
    w_all = jnp.concatenate([
        _pad_head_rows(wv, DV_C, DV_P), _pad_head_rows(wr, DV_C, DV_P),
        _pad_head_rows(wq, DK_C, DK_P), _pad_head_rows(wk, DK_C, DK_P),
        wt[:, 3 * D_A:c0],
        jnp.pad(wg, ((0, 0), (0, 2 * LANE - GATE_RANK), (0, 0))),
        wt[:, :3 * D_A],
    ], axis=1).astype(BF16)
    assert w_all.shape[1] == N_SLABS * LANE
    w2p = jnp.pad(_pad_heads(w_gate2, DK_C, DK_P), ((0, 0), (0, LANE - GATE_RANK), (0, 0)))
    bgp = _pad_heads(b_gate, DK_C, DK_P)[:, None, :]
    gnp = jnp.pad(gla_norm_g, ((0, 0), (0, DV_P - DV_C)))[:, None, :]
    wo_c = _pad_head_rows(w_o[:, D_A + D_B:], DV_C, DV_P)
    return w_all, w2p, bgp, gnp, wo_c


def _state_to_kernel(s):
    s = jnp.swapaxes(s, 2, 3)
    return jnp.pad(s, ((0, 0), (0, 0), (0, DV_P - DV_C), (0, DK_P - DK_C)))


def _state_from_kernel(s):
    return jnp.swapaxes(s[:, :, :DV_C, :DK_C], 2, 3)


def _heads_to_rows(a):
    return jnp.transpose(a, (0, 1, 3, 2, 4))


def _set_rows(big, row0, small):
    return lax.dynamic_update_slice(big, small.astype(big.dtype), (row0, 0))


def kernel(x_prompt, x_sample, cache_win_k, cache_win_v, state_pool, state_gla, norm1_g, w_in, w_gate2, b_gate,
           w_pool, pool_scale, gla_norm_g, w_o, norm2_g, w_ffn_gate, w_ffn_up, w_ffn_down, final_norm_g):
    bp, sp, d_model = x_prompt.shape
    bs, ts, _ = x_sample.shape
    depth = w_in.shape[0]
    d_ff = w_ffn_down.shape[1]
    mp, ms = bp * sp, bs * ts
    m_all = mp + ms
    assert min(WIN_MAX, sp) == WIN_MAX
    assert d_ff % 2 == 0

    w_all, w2p, bgp, gnp, wo_c = _relayout_weights(w_in, w_gate2, b_gate, gla_norm_g, w_o)
    cache_k = jnp.transpose(cache_win_k, (0, 1, 3, 2, 4))
    cache_v = jnp.transpose(cache_win_v, (0, 1, 3, 2, 4))
    pool_scale3 = pool_scale[:, None, :]

    x = jnp.concatenate([x_prompt.reshape(mp, d_model), x_sample.reshape(ms, d_model)], axis=0)
    zero_hist = jnp.zeros((bp, HIST_ROWS, D_B), F32)
    zero_state = jnp.zeros((bp, H_C, DV_P, DK_P), F32)
    p_pool, p_gla, s_k, s_v, s_pool, s_gla = [], [], [], [], [], []
    win_k = win_v = None
    for l in range(depth):
        slabs = _proj(x, norm1_g[l], w_all, l)

        out_a, win_k, win_v = _span_attn(slabs, bp, sp, l, depth, win_k, win_v)
        out_b = _pool(slabs, 0, m_all, zero_hist, w_pool, pool_scale3, l, bp, sp, 0)
        out_c, st_p = _gla(slabs, 0, m_all, w2p, bgp, gnp, zero_state, l, bp, sp)
        u_p = jnp.stack([slabs[SLAB_U:SLAB_U + N_POOL, (b + 1) * sp - POOL_BUF:(b + 1) * sp] for b in range(bp)])
        p_pool.append(jnp.transpose(u_p, (0, 2, 1, 3)).reshape(bp, POOL_BUF, D_B))
        p_gla.append(_state_from_kernel(st_p))

        hist = jnp.pad(state_pool[l], ((0, 0), (HIST_ROWS - POOL_BUF, 0), (0, 0)))
        sa = _dec_attn(slabs, mp, cache_k, cache_v, l, bs, ts)
        sb = _pool(slabs, mp, ms, hist, w_pool, pool_scale3, l, bs, ts, PAST_LEN)
        sc, st_s = _gla(slabs, mp, ms, w2p, bgp, gnp, _state_to_kernel(state_gla[l]), l, bs, ts)
        out_a = _set_rows(out_a, mp, sa)
        out_b = _set_rows(out_b, mp, sb)
        out_c = _set_rows(out_c, mp, sc)
        new_kv = slabs[SLAB_K:SLAB_K + 2 * H_A, mp:].reshape(2, H_A, bs, ts, HD_A)
        s_k.append(jnp.swapaxes(new_kv[0], 0, 1))
        s_v.append(jnp.swapaxes(new_kv[1], 0, 1))
        u_s = jnp.transpose(slabs[SLAB_U:SLAB_U + N_POOL, mp:].reshape(N_POOL, bs, ts, LANE), (1, 2, 0, 3))
        u_ext = jnp.concatenate([state_pool[l], u_s.reshape(bs, ts, D_B)], axis=1)
        s_pool.append(u_ext[:, -POOL_BUF:])
        s_gla.append(_state_from_kernel(st_s))

        x = _matmul_res([(w_o, 0, [(out_a, D_A, 0), (out_b, D_B, 0)]), (wo_c, 0, [(out_c, H_C * DV_P, 0)])],
                        l, x, "out_proj")
        h2 = _rmsnorm(x, norm2_g[l], BF16)
        f = _glu(h2, w_ffn_gate, w_ffn_up, l)
        half = d_ff // 2
        x = _matmul_res([(w_ffn_down, 0, [(f, half, 0)])], l, x, "ffn_down")
        x = _matmul_res([(w_ffn_down, 1, [(f, half, 1)])], l, x, "ffn_down")

    y_prompt = _rmsnorm(x, final_norm_g, F32, row0=0, rows=mp).reshape(bp, sp, d_model)
    y_sample = _rmsnorm(x, final_norm_g, F32, row0=mp, rows=ms).reshape(bs, ts, d_model)
    return (y_prompt, y_sample, _heads_to_rows(win_k), _heads_to_rows(win_v),
            jnp.stack(p_pool), jnp.stack(p_gla), _heads_to_rows(jnp.stack(s_k)), _heads_to_rows(jnp.stack(s_v)),
            jnp.stack(s_pool), jnp.stack(s_gla))
```

```python
import functools

import jax
import jax.numpy as jnp
from jax import lax
from jax.experimental import pallas as pl
from jax.experimental.pallas import tpu as pltpu

F32 = jnp.float32
BF16 = jnp.bfloat16

H_A, HD_A = 6, 128
D_A = H_A * HD_A
DILATED_PATTERNS = ((128, 1), (512, 4), (2048, 16))
WIN_MAX = 2048
POOL_WINDOWS = (2, 4, 8, 16)
N_POOL = 4
D_B = 512
POOL_GROUP = D_B // N_POOL
POOL_BUF = 15
H_C, DK_C, DV_C = 4, 96, 192
D_C = H_C * DV_C
GATE_RANK = 16
GATE_NORMALIZER = 16.0
GLA_CHUNK = 32
PAST_LEN = 16384
EPS = 1e-6
NEG_INF = -1e30
LOG2E = 1.4426950408889634

LANE = 128
SUBLANE = 8
PACK = 16
VMEM_LIMIT_BYTES = 56 * 1024 * 1024

DK_P = 128
DV_P = 256
HIST_ROWS = 16

VS = DV_P // LANE
SLAB_VC = 0
SLAB_RC = SLAB_VC + H_C * VS
SLAB_QC = SLAB_RC + H_C * VS
SLAB_KC = SLAB_QC + H_C
SLAB_U = SLAB_KC + H_C
SLAB_G = SLAB_U + N_POOL
SLAB_Q = SLAB_G + 2
SLAB_K = SLAB_Q + H_A
SLAB_V = SLAB_K + H_A
N_SLABS = SLAB_V + H_A
D_MIX_P = D_A + D_B + H_C * DV_P

ATTN_BLK = 128
assert all(w // d == ATTN_BLK and d & (d - 1) == 0 for w, d in DILATED_PATTERNS)

TM_ROWS = 1376
TM_NORM = 688
TN_PROJ = 768
TN_WIDE = 512
TM_RES = 688
TN_RES = 1024
GLA_GROUP = 8


def _params(*sem):
    return pltpu.CompilerParams(dimension_semantics=sem, vmem_limit_bytes=VMEM_LIMIT_BYTES)


def _row_tile(m, target, mult=PACK):
    if m <= target:
        return m
    t = target - target % mult
    while t >= mult:
        if m % t == 0:
            return t
        t -= mult
    raise ValueError(f"no row tile for {m}")


def _rmsnorm_kernel(x_ref, g_ref, o_ref):
    x = x_ref[...]
    ms = jnp.mean(x * x, axis=-1, keepdims=True)
    o_ref[...] = (x * lax.rsqrt(ms + EPS) * g_ref[...]).astype(o_ref.dtype)


def _rmsnorm(x, g, out_dtype, *, row0=0, rows=None):
    d = x.shape[1]
    rows = x.shape[0] if rows is None else rows
    tm = _row_tile(rows, TM_NORM)
    assert row0 % tm == 0
    return pl.pallas_call(
        _rmsnorm_kernel,
        grid=(rows // tm,),
        in_specs=[pl.BlockSpec((tm, d), lambda i: (i + row0 // tm, 0)), pl.BlockSpec((1, d), lambda i: (0, 0))],
        out_specs=pl.BlockSpec((tm, d), lambda i: (i, 0)),
        out_shape=jax.ShapeDtypeStruct((rows, d), out_dtype),
        compiler_params=_params("arbitrary"),
        name="rmsnorm",
    )(x, g.reshape(1, d))


def _inv_rms(ss_ref, rows, width):
    tot = ss_ref[0, rows, :]
    for s in range(1, ss_ref.shape[0]):
        tot = tot + ss_ref[s, rows, :]
    return lax.rsqrt(tot * (1.0 / width) + EPS)


def _norm_prep_kernel(x_ref, g_ref, xg_ref, ss_ref):
    x = x_ref[...]
    xg_ref[...] = (x * g_ref[...]).astype(BF16)
    total = jnp.broadcast_to(jnp.sum(x * x, axis=-1, keepdims=True), (x.shape[0], LANE))
    ss_ref[0] = total
    for s in range(1, ss_ref.shape[0]):
        ss_ref[s] = jnp.zeros_like(total)


def _norm_prep(x, g):
    m, d = x.shape
    tm = _row_tile(m, TM_NORM)
    return pl.pallas_call(
        _norm_prep_kernel,
        grid=(m // tm,),
        in_specs=[pl.BlockSpec((tm, d), lambda i: (i, 0)), pl.BlockSpec((1, d), lambda i: (0, 0))],
        out_specs=[pl.BlockSpec((tm, d), lambda i: (i, 0)), pl.BlockSpec((d // TN_RES, tm, LANE), lambda i: (0, i, 0))],
        out_shape=[jax.ShapeDtypeStruct((m, d), BF16), jax.ShapeDtypeStruct((d // TN_RES, m, LANE), F32)],
        compiler_params=_params("arbitrary"),
        name="norm_prep",
    )(x, g.reshape(1, d))


def _proj_kernel(x_ref, ss_ref, w_ref, o_ref):
    acc = lax.dot_general(x_ref[...], w_ref[...], (((1,), (1,)), ((), ())), preferred_element_type=F32)
    r = _inv_rms(ss_ref, slice(None), x_ref.shape[1])
    for c in range(o_ref.shape[0]):
        o_ref[c] = acc[:, c * LANE:(c + 1) * LANE] * r


def _proj(xg, ss, w_all, layer):
    m, k = xg.shape
    n = w_all.shape[1]
    tm = _row_tile(m, TM_ROWS)
    tn = TN_PROJ
    assert n % tn == 0
    return pl.pallas_call(
        _proj_kernel,
        grid=(m // tm, n // tn),
        in_specs=[pl.BlockSpec((tm, k), lambda i, j: (i, 0)),
                  pl.BlockSpec((ss.shape[0], tm, LANE), lambda i, j: (0, i, 0)),
                  pl.BlockSpec((None, tn, k), lambda i, j: (layer, j, 0))],
        out_specs=pl.BlockSpec((tn // LANE, tm, LANE), lambda i, j: (j, i, 0)),
        out_shape=jax.ShapeDtypeStruct((n // LANE, m, LANE), F32),
        compiler_params=_params("arbitrary", "arbitrary"),
        name="proj_in",
    )(xg, ss, w_all)


def _mm_res_kernel(*refs, x_counts, emit_norm):
    n_x, n_w = sum(x_counts), len(x_counts)
    xs = refs[:n_x]
    ws = refs[n_x:n_x + n_w]
    pos = n_x + n_w
    res_ref = refs[pos]
    gain_ref = refs[pos + 1] if emit_norm else None
    pos += 1 + int(emit_norm)
    o_ref = refs[pos]
    xg_ref, ss_ref = (refs[pos + 1], refs[pos + 2]) if emit_norm else (None, None)
    wbs = refs[pos + 1 + 2 * int(emit_norm):]

    @pl.when(pl.program_id(1) == 0)
    def _():
        for w_ref, wb_ref in zip(ws, wbs):
            wb_ref[...] = w_ref[...].astype(BF16)

    acc = res_ref[...]
    xi = 0
    for wb_ref, count in zip(wbs, x_counts):
        k0 = 0
        for x_ref in xs[xi:xi + count]:
            k = x_ref.shape[1]
            acc = acc + jnp.dot(x_ref[...], wb_ref[k0:k0 + k, :], preferred_element_type=F32)
            k0 += k
        xi += count
    o_ref[...] = acc
    if emit_norm:
        xg_ref[...] = (acc * gain_ref[...]).astype(BF16)
        ss_ref[0] = jnp.broadcast_to(jnp.sum(acc * acc, axis=-1, keepdims=True), (acc.shape[0], LANE))


def _matmul_res(parts, layer, res, name, norm_gain=None):
    m, n = res.shape
    tm = _row_tile(m, TM_RES)
    tn = TN_RES
    assert n % tn == 0
    emit_norm = norm_gain is not None
    x_specs, w_specs, xs, ws, scratch = [], [], [], [], []
    for w, rblk, x_list in parts:
        ktot = sum(width for _, width, _ in x_list)
        for x, width, cb in x_list:
            xs.append(x)
            x_specs.append(pl.BlockSpec((tm, width), lambda j, i, cb=cb: (i, cb)))
        ws.append(w)
        w_specs.append(pl.BlockSpec((None, ktot, tn), lambda j, i, rblk=rblk: (layer, rblk, j)))
        scratch.append(pltpu.VMEM((ktot, tn), BF16))
    tile = pl.BlockSpec((tm, tn), lambda j, i: (i, j))
    in_specs = x_specs + w_specs + [tile]
    args = [*xs, *ws, res]
    out_specs, out_shape = [tile], [jax.ShapeDtypeStruct((m, n), F32)]
    if emit_norm:
        in_specs.append(pl.BlockSpec((1, tn), lambda j, i: (0, j)))
        args.append(norm_gain.reshape(1, n))
        out_specs += [tile, pl.BlockSpec((1, tm, LANE), lambda j, i: (j, i, 0))]
        out_shape += [jax.ShapeDtypeStruct((m, n), BF16), jax.ShapeDtypeStruct((n // tn, m, LANE), F32)]
    out = pl.pallas_call(
        functools.partial(_mm_res_kernel, x_counts=tuple(len(p[2]) for p in parts), emit_norm=emit_norm),
        grid=(n // tn, m // tm),
        in_specs=in_specs,
        out_specs=out_specs,
        out_shape=out_shape,
        scratch_shapes=scratch,
        input_output_aliases={len(xs) + len(ws): 0},
        compiler_params=_params("arbitrary", "arbitrary"),
        name=name,
    )(*args)
    return out if emit_norm else out[0]


def _glu_kernel(x_ref, ss_ref, wg_ref, wu_ref, o_ref, wgb_ref, wub_ref, *, row_split):
    @pl.when(pl.program_id(1) == 0)
    def _():
        wgb_ref[...] = wg_ref[...].astype(BF16)
        wub_ref[...] = wu_ref[...].astype(BF16)

    rows = x_ref.shape[0] // row_split
    reps = o_ref.shape[1] // LANE
    for r in range(row_split):
        rs = slice(r * rows, (r + 1) * rows)
        x = x_ref[rs, :]
        inv = _inv_rms(ss_ref, rs, x_ref.shape[1])
        inv = jnp.concatenate([inv] * reps, axis=-1)
        g = jnp.dot(x, wgb_ref[...], preferred_element_type=F32) * inv
        u = jnp.dot(x, wub_ref[...], preferred_element_type=F32) * inv
        o_ref[rs, :] = (g * (1.0 / (1.0 + jnp.exp(-g))) * u).astype(o_ref.dtype)


def _glu(xg, ss, wg, wu, layer):
    m, k = xg.shape
    n = wg.shape[2]
    tm = _row_tile(m, TM_ROWS)
    tn = TN_WIDE
    row_split = 2 if tm % (2 * PACK) == 0 else 1
    assert n % tn == 0
    w_spec = pl.BlockSpec((None, k, tn), lambda j, i: (layer, 0, j))
    return pl.pallas_call(
        functools.partial(_glu_kernel, row_split=row_split),
        grid=(n // tn, m // tm),
        in_specs=[pl.BlockSpec((tm, k), lambda j, i: (i, 0)),
                  pl.BlockSpec((ss.shape[0], tm, LANE), lambda j, i: (0, i, 0)), w_spec, w_spec],
        out_specs=pl.BlockSpec((tm, tn), lambda j, i: (i, j)),
        out_shape=jax.ShapeDtypeStruct((m, n), BF16),
        scratch_shapes=[pltpu.VMEM((k, tn), BF16), pltpu.VMEM((k, tn), BF16)],
        compiler_params=_params("arbitrary", "arbitrary"),
        name="ffn_glu",
    )(xg, ss, wg, wu)


def _span_attn_kernel(q_ref, k_ref, v_ref, *rest, skew, n_carried):
    o_ref, pk_ref, pv_ref, k_scr, v_scr, o_scr, l_scr, bias_scr = rest[n_carried:]
    n0 = pl.program_id(2)
    span = q_ref.shape[0]
    blk = ATTN_BLK
    units = span // blk
    scale = HD_A ** -0.5
    cur = (n0 % 2) * span
    prev = span - cur

    k_scr[pl.ds(cur, span), :] = k_ref[...]
    v_scr[pl.ds(cur, span), :] = v_ref[...]

    @pl.when(n0 == pl.num_programs(2) - 1)
    def _():
        pk_ref[...] = k_ref[...]
        pv_ref[...] = v_ref[...]

    qi = lax.broadcasted_iota(jnp.int32, (blk, blk), 0)
    ki = lax.broadcasted_iota(jnp.int32, (blk, blk), 1)
    bias_scr[0] = jnp.where(ki <= qi, 0.0, NEG_INF)
    bias_scr[1] = jnp.where(ki >= qi, 0.0, NEG_INF)
    nt = (((1,), (1,)), ((), ()))

    def rows(start, dil):
        return pl.ds(start, blk) if dil == 1 else pl.ds(start, blk, stride=dil)

    def unit_starts(u, dil):
        r, nb = u % dil, u // dil
        start = r + dil * blk * nb
        prev_start = cur + start - dil * blk if nb > 0 else prev + r + span - dil * blk
        return start, prev_start, nb

    def run_units(first_span):
        def scores(u, dil):
            start, prev_start, nb = unit_starts(u, dil)
            q = (q_ref[rows(start, dil), :] * (scale * LOG2E)).astype(BF16)
            kc = k_scr[rows(cur + start, dil), :].astype(BF16)
            sc = lax.dot_general(q, kc, nt, preferred_element_type=F32) + bias_scr[0]
            if first_span and nb == 0:
                return None, sc
            kp = k_scr[rows(prev_start, dil), :].astype(BF16)
            sp = lax.dot_general(q, kp, nt, preferred_element_type=F32) + bias_scr[1]
            return sp, sc

        def finish(p, u, dil, sp, sc):
            start, prev_start, _ = unit_starts(u, dil)
            vc = v_scr[rows(cur + start, dil), :].astype(BF16)
            if sp is None:
                m = jnp.max(sc, axis=-1, keepdims=True)
                pc = jnp.exp2(sc - m)
                den = jnp.sum(pc, axis=-1, keepdims=True)
                o = jnp.dot(pc.astype(BF16), vc, preferred_element_type=F32)
            else:
                vp = v_scr[rows(prev_start, dil), :].astype(BF16)
                m = jnp.max(jnp.maximum(sp, sc), axis=-1, keepdims=True)
                pp = jnp.exp2(sp - m)
                pc = jnp.exp2(sc - m)
                den = jnp.sum(pp + pc, axis=-1, keepdims=True)
                o = jnp.dot(pp.astype(BF16), vp, preferred_element_type=F32)
                o = o + jnp.dot(pc.astype(BF16), vc, preferred_element_type=F32)
            o_scr[p, rows(start, dil), :] = o / den
            l_scr[p, rows(start, dil), :] = jnp.broadcast_to(m + jnp.log(den) * LOG2E, (blk, HD_A))

        order = [(p, u, dil) for p, (_, dil) in enumerate(DILATED_PATTERNS) for u in range(units)]
        pending = []
        for step in range(len(order) + skew):
            if step < len(order):
                p, u, dil = order[step]
                pending.append((p, u, dil) + scores(u, dil))
            if step >= skew:
                finish(*pending.pop(0))

    @pl.when(n0 == 0)
    def _():
        run_units(True)

    @pl.when(n0 > 0)
    def _():
        run_units(False)

    ct = 64

    def combine(c, carry):
        rs = pl.ds(pl.multiple_of(c * ct, ct), ct)
        l1, l2, l3 = l_scr[0, rs, :], l_scr[1, rs, :], l_scr[2, rs, :]
        mx = jnp.maximum(jnp.maximum(l1, l2), l3)
        e1, e2, e3 = jnp.exp2(l1 - mx), jnp.exp2(l2 - mx), jnp.exp2(l3 - mx)
        num = e1 * o_scr[0, rs, :] + e2 * o_scr[1, rs, :] + e3 * o_scr[2, rs, :]
        o_ref[rs, :] = (num / (e1 + e2 + e3)).astype(o_ref.dtype)
        return carry

    lax.fori_loop(0, span // ct, combine, 0, unroll=2)


def _span_attn(slabs, batch, seq, layer, depth, win_k, win_v):
    span = WIN_MAX
    assert seq % span == 0 and len(DILATED_PATTERNS) == 3
    nsp = seq // span
    slab = lambda s0: pl.BlockSpec((None, span, HD_A), lambda b, h, n: (s0 + h, b * nsp + n, 0))
    win_spec = pl.BlockSpec((None, None, None, span, HD_A), lambda b, h, n: (layer, b, h, 0, 0))
    win_shape = jax.ShapeDtypeStruct((depth, batch, H_A, span, HD_A), F32)
    in_specs = [slab(SLAB_Q), slab(SLAB_K), slab(SLAB_V)]
    args = [slabs, slabs, slabs]
    aliases = {}
    if win_k is not None:
        in_specs += [pl.BlockSpec(memory_space=pl.ANY)] * 2
        args += [win_k, win_v]
        aliases = {3: 1, 4: 2}
    skew = 6
    return pl.pallas_call(
        functools.partial(_span_attn_kernel, skew=skew, n_carried=len(aliases)),
        grid=(batch, H_A, nsp),
        in_specs=in_specs,
        out_specs=[pl.BlockSpec((span, HD_A), lambda b, h, n: (b * nsp + n, h)), win_spec, win_spec],
        out_shape=[jax.ShapeDtypeStruct((slabs.shape[1], D_A), BF16), win_shape, win_shape],
        scratch_shapes=[pltpu.VMEM((2 * span, HD_A), F32), pltpu.VMEM((2 * span, HD_A), F32),
                        pltpu.VMEM((3, span, HD_A), F32), pltpu.VMEM((3, span, HD_A), F32),
                        pltpu.VMEM((2, ATTN_BLK, ATTN_BLK), F32)],
        input_output_aliases=aliases,
        compiler_params=_params("arbitrary", "arbitrary", "arbitrary"),
        name="span_attn",
    )(*args)


def _dec_attn_kernel(q_ref, kn_ref, vn_ref, kc_ref, vc_ref, o_ref, *, past):
    t = q_ref.shape[1]
    scale = HD_A ** -0.5
    nt = (((1,), (1,)), ((), ()))
    d_c = past + lax.broadcasted_iota(jnp.int32, (t, past), 0) - lax.broadcasted_iota(jnp.int32, (t, past), 1)
    d_n = lax.broadcasted_iota(jnp.int32, (t, t), 0) - lax.broadcasted_iota(jnp.int32, (t, t), 1)
    masks = []
    for window, dil in DILATED_PATTERNS:
        m_c = jnp.logical_and((d_c & (dil - 1)) == 0, d_c <= window)
        m_n = jnp.logical_and(jnp.logical_and(d_n >= 0, (d_n & (dil - 1)) == 0), d_n <= window)
        masks.append((m_c, m_n))
    for h in range(H_A):
        q = q_ref[h]
        vc = vc_ref[h]
        vn = vn_ref[h]
        s_c = lax.dot_general(q, kc_ref[h], nt, preferred_element_type=F32) * scale
        s_n = lax.dot_general(q, kn_ref[h], nt, preferred_element_type=F32) * scale
        outs, lses = [], []
        for m_c, m_n in masks:
            a_c = jnp.where(m_c, s_c, NEG_INF)
            a_n = jnp.where(m_n, s_n, NEG_INF)
            m = jnp.maximum(jnp.max(a_c, axis=-1, keepdims=True), jnp.max(a_n, axis=-1, keepdims=True))
            p_c = jnp.exp(a_c - m)
            p_n = jnp.exp(a_n - m)
            den = jnp.sum(p_c, axis=-1, keepdims=True) + jnp.sum(p_n, axis=-1, keepdims=True)
            o = jnp.dot(p_c, vc, preferred_element_type=F32) + jnp.dot(p_n, vn, preferred_element_type=F32)
            outs.append(o / den)
            lses.append(m + jnp.log(den))
        mx = jnp.maximum(jnp.maximum(lses[0], lses[1]), lses[2])
        es = [jnp.exp(l - mx) for l in lses]
        num = es[0] * outs[0] + es[1] * outs[1] + es[2] * outs[2]
        o_ref[:, h * HD_A:(h + 1) * HD_A] = (num / (es[0] + es[1] + es[2])).astype(o_ref.dtype)


def _dec_attn(slabs, row0, cache_k, cache_v, layer, batch, t):
    past = cache_k.shape[3]
    assert row0 % t == 0 and SLAB_Q % H_A == 0 and SLAB_K % H_A == 0 and SLAB_V % H_A == 0
    new_spec = lambda s0: pl.BlockSpec((H_A, t, HD_A), lambda b: (s0 // H_A, row0 // t + b, 0))
    cache_spec = pl.BlockSpec((None, None, H_A, past, HD_A), lambda b: (layer, b, 0, 0, 0))
    return pl.pallas_call(
        functools.partial(_dec_attn_kernel, past=past),
        grid=(batch,),
        in_specs=[new_spec(SLAB_Q), new_spec(SLAB_K), new_spec(SLAB_V), cache_spec, cache_spec],
        out_specs=pl.BlockSpec((t, D_A), lambda b: (b, 0)),
        out_shape=jax.ShapeDtypeStruct((batch * t, D_A), F32),
        compiler_params=_params("arbitrary"),
        name="dec_attn",
    )(slabs, slabs, slabs, cache_k, cache_v)


def _pool_kernel(*refs, pos0, has_prev):
    if has_prev:
        u_ref, prev_ref, hist_ref, w_ref, sc_ref, o_ref, ext_ref = refs
    else:
        u_ref, hist_ref, w_ref, sc_ref, o_ref, ext_ref = refs
        prev_ref = None
    i = pl.program_id(1)
    tm = u_ref.shape[1]

    @pl.when(i == 0)
    def _():
        for gi in range(N_POOL):
            ext_ref[gi, 0:HIST_ROWS, :] = hist_ref[0, :, gi * POOL_GROUP:(gi + 1) * POOL_GROUP]

    if has_prev:
        @pl.when(i > 0)
        def _():
            ext_ref[:, 0:HIST_ROWS, :] = prev_ref[...]

    ext_ref[:, HIST_ROWS:HIST_ROWS + tm, :] = u_ref[...]
    pos = pos0 + i * tm + lax.broadcasted_iota(jnp.int32, (tm, POOL_GROUP), 0)
    for gi, w in enumerate(POOL_WINDOWS):
        cs = slice(gi * POOL_GROUP, (gi + 1) * POOL_GROUP)
        tok = ext_ref[gi, HIST_ROWS:HIST_ROWS + tm, :]
        tot = tok
        for back in range(1, w):
            tot = tot + ext_ref[gi, HIST_ROWS - back:HIST_ROWS - back + tm, :]
        cnt = jnp.minimum(w, pos + 1).astype(F32)
        pooled = tot / cnt - tok
        out = jnp.dot(pooled.astype(BF16), w_ref[gi].astype(BF16), preferred_element_type=F32)
        o_ref[:, cs] = (out * sc_ref[:, cs]).astype(o_ref.dtype)


def _pool(slabs, row0, out_rows, hist, w_pool, scale, layer, batch, t, pos0):
    tm = _row_tile(t, 512)
    nt = t // tm
    has_prev = nt > 1
    assert row0 % tm == 0
    rb = lambda b, i: row0 // tm + b * nt + i
    ob = lambda b, i: b * nt + i
    in_specs = [pl.BlockSpec((N_POOL, tm, LANE), lambda b, i: (SLAB_U // N_POOL, rb(b, i), 0))]
    args = [slabs]
    if has_prev:
        per = tm // HIST_ROWS
        in_specs.append(pl.BlockSpec((N_POOL, HIST_ROWS, LANE),
                                     lambda b, i: (SLAB_U // N_POOL, jnp.maximum(rb(b, i) * per - 1, 0), 0)))
        args.append(slabs)
    in_specs += [
        pl.BlockSpec((1, HIST_ROWS, D_B), lambda b, i: (b, 0, 0)),
        pl.BlockSpec((None, N_POOL, POOL_GROUP, POOL_GROUP), lambda b, i: (layer, 0, 0, 0)),
        pl.BlockSpec((None, 1, D_B), lambda b, i: (layer, 0, 0)),
    ]
    args += [hist, w_pool, scale]
    return pl.pallas_call(
        functools.partial(_pool_kernel, pos0=pos0, has_prev=has_prev),
        grid=(batch, nt),
        in_specs=in_specs,
        out_specs=pl.BlockSpec((tm, D_B), lambda b, i: (ob(b, i), 0)),
        out_shape=jax.ShapeDtypeStruct((out_rows, D_B), BF16 if tm % PACK == 0 else F32),
        scratch_shapes=[pltpu.VMEM((N_POOL, HIST_ROWS + tm, LANE), F32)],
        compiler_params=_params("arbitrary", "arbitrary"),
        name="pool_mix",
    )(*args)


def _gla_kernel(q_ref, k_ref, v_ref, r_ref, ga_ref, w2_ref, bg_ref, gn_ref, s0_ref, o_ref, s_out_ref,
                st_ref, g_ref, *, chunk, group):
    i = pl.program_id(1)
    ts = q_ref.shape[1]

    @pl.when(i == 0)
    def _():
        st_ref[...] = s0_ref[0]

    z = jnp.dot(ga_ref[...], w2_ref[...], preferred_element_type=F32) + bg_ref[...]
    g_ref[...] = (jnp.minimum(z, 0.0) - jnp.log(1.0 + jnp.exp(-jnp.abs(z)))) / GATE_NORMALIZER

    gr = group * chunk
    ri = lax.broadcasted_iota(jnp.int32, (gr, gr), 0)
    ci = lax.broadcasted_iota(jnp.int32, (gr, gr), 1)
    same_chunk = (ri // chunk) == (ci // chunk)
    tril = jnp.logical_and(same_chunk, ci <= ri)
    sum_mat = jnp.concatenate([tril.astype(F32), same_chunk.astype(F32)], axis=0)
    split_sums = gr % PACK == 0
    qscale = DK_C ** -0.5
    nt = (((1,), (1,)), ((), ()))
    tn = (((0,), (0,)), ((), ()))

    def body(gi, carry):
        rows = pl.ds(pl.multiple_of(gi * gr, gr), gr)
        if split_sums:
            rem = g_ref[rows, :]
            sums = None
            for _ in range(3):
                part = rem.astype(BF16)
                rem = rem - part.astype(F32)
                d = jnp.dot(sum_mat.astype(BF16), part, preferred_element_type=F32)
                sums = d if sums is None else sums + d
        else:
            sums = jnp.dot(sum_mat, g_ref[rows, :], preferred_element_type=F32, precision=lax.Precision.HIGHEST)
        b = sums[:gr]
        bl = sums[gr:]
        eb = jnp.exp(b)
        enb = jnp.exp(-b)
        ebl = jnp.exp(bl)
        ekd = jnp.exp(bl - b)
        heads = range(H_C)
        ksl = [slice(h * DK_P, (h + 1) * DK_P) for h in heads]
        vh = [jnp.concatenate([v_ref[VS * h + s, rows, :] for s in range(VS)], axis=-1).astype(BF16) for h in heads]
        qe = [(q_ref[h, rows, :] * qscale * eb[:, ksl[h]]).astype(BF16) for h in heads]
        ke = [(k_ref[h, rows, :] * enb[:, ksl[h]]).astype(BF16) for h in heads]
        kd = [(k_ref[h, rows, :] * ekd[:, ksl[h]]).astype(BF16) for h in heads]
        a = [lax.dot_general(qe[h], ke[h], nt, preferred_element_type=F32) for h in heads]
        st = [st_ref[h] for h in heads]
        inter = [[] for _ in heads]
        for c in range(group):
            cr = slice(c * chunk, (c + 1) * chunk)
            kv = [lax.dot_general(vh[h][cr], kd[h][cr], tn, preferred_element_type=F32) for h in heads]
            for h in heads:
                inter[h].append(lax.dot_general(qe[h][cr], st[h].astype(BF16), nt,
                                                preferred_element_type=F32))
            st = [st[h] * ebl[c * chunk:c * chunk + 1, ksl[h]] + kv[h] for h in heads]
        for h in heads:
            st_ref[h] = st[h]
            am = jnp.where(tril, a[h], 0.0).astype(BF16)
            o = jnp.dot(am, vh[h], preferred_element_type=F32)
            o = o + (inter[h][0] if group == 1 else jnp.concatenate(inter[h], axis=0))
            ms = jnp.sum(o * o, axis=-1, keepdims=True) * (1.0 / DV_C)
            on = o * lax.rsqrt(ms + EPS) * gn_ref[...]
            rh = jnp.concatenate([r_ref[VS * h + s, rows, :] for s in range(VS)], axis=-1)
            o_ref[rows, h * DV_P:(h + 1) * DV_P] = (on * (rh * (1.0 / (1.0 + jnp.exp(-rh))))).astype(o_ref.dtype)
        return carry

    lax.fori_loop(0, ts // gr, body, 0, unroll=2)

    @pl.when(i == pl.num_programs(1) - 1)
    def _():
        s_out_ref[0] = st_ref[...]


def _gla(slabs, row0, out_rows, w2p, bgp, gnp, s0t, layer, batch, t):
    chunk = min(GLA_CHUNK, t)
    out_dtype = BF16 if chunk % PACK == 0 else F32
    ts = _row_tile(t, 512)
    nt = t // ts
    group = GLA_GROUP if ts % (GLA_GROUP * chunk) == 0 else 1
    assert t % chunk == 0 and ts % chunk == 0 and row0 % ts == 0
    qk_w, v_w = H_C * DK_P, H_C * DV_P
    rb = lambda b, i: row0 // ts + b * nt + i
    grp = lambda n, s0: pl.BlockSpec((n, ts, LANE), lambda b, i: (s0 // n, rb(b, i), 0))
    assert SLAB_VC % (H_C * VS) == 0 and SLAB_RC % (H_C * VS) == 0 and SLAB_QC % H_C == 0 and SLAB_KC % H_C == 0
    par = lambda shape: pl.BlockSpec((None,) + shape, lambda b, i: (layer,) + (0,) * len(shape))
    state_spec = pl.BlockSpec((1, H_C, DV_P, DK_P), lambda b, i: (b, 0, 0, 0))
    return pl.pallas_call(
        functools.partial(_gla_kernel, chunk=chunk, group=group),
        grid=(batch, nt),
        in_specs=[grp(H_C, SLAB_QC), grp(H_C, SLAB_KC), grp(H_C * VS, SLAB_VC), grp(H_C * VS, SLAB_RC),
                  pl.BlockSpec((None, ts, LANE), lambda b, i: (SLAB_G, rb(b, i), 0)),
                  par((LANE, qk_w)), par((1, qk_w)), par((1, DV_P)), state_spec],
        out_specs=[pl.BlockSpec((ts, v_w), lambda b, i: (b * nt + i, 0)), state_spec],
        out_shape=[jax.ShapeDtypeStruct((out_rows, v_w), out_dtype),
                   jax.ShapeDtypeStruct((batch, H_C, DV_P, DK_P), F32)],
        scratch_shapes=[pltpu.VMEM((H_C, DV_P, DK_P), F32), pltpu.VMEM((ts, qk_w), F32)],
        compiler_params=_params("arbitrary", "arbitrary"),
        name="gla",
    )(slabs, slabs, slabs, slabs, slabs, w2p, bgp, gnp, s0t)


def _pad_head_rows(w, width, padded):
    depth, _, k = w.shape
    w = w.reshape(depth, H_C, width, k)
    w = jnp.pad(w, ((0, 0), (0, 0), (0, padded - width), (0, 0)))
    return w.reshape(depth, H_C * padded, k)


def _pad_heads(w, width, padded):
    lead = w.shape[:-1]
    w = w.reshape(lead + (H_C, width))
    w = jnp.pad(w, [(0, 0)] * len(lead) + [(0, 0), (0, padded - width)])
    return w.reshape(lead + (H_C * padded,))


def _relayout_weights(w_in, w_gate2, b_gate, gla_norm_g, w_o):
    wt = jnp.swapaxes(w_in, 1, 2)
    c0 = 3 * D_A + D_B
    qk = H_C * DK_C
    wq = wt[:, c0:c0 + qk]
    wk = wt[:, c0 + qk:c0 + 2 * qk]
    wv = wt[:, c0 + 2 * qk:c0 + 2 * qk + D_C]
    wr = wt[:, c0 + 2 * qk + D_C:c0 + 2 * qk + 2 * D_C]
    wg = wt[:, c0 + 2 * qk + 2 * D_C:]
    w_all = jnp.concatenate([
        _pad_head_rows(wv, DV_C, DV_P), _pad_head_rows(wr, DV_C, DV_P),
        _pad_head_rows(wq, DK_C, DK_P), _pad_head_rows(wk, DK_C, DK_P),
        wt[:, 3 * D_A:c0],
        jnp.pad(wg, ((0, 0), (0, 2 * LANE - GATE_RANK), (0, 0))),
        wt[:, :3 * D_A],
    ], axis=1).astype(BF16)
    assert w_all.shape[1] == N_SLABS * LANE
    w2p = jnp.pad(_pad_heads(w_gate2, DK_C, DK_P), ((0, 0), (0, LANE - GATE_RANK), (0, 0)))
    bgp = _pad_heads(b_gate, DK_C, DK_P)[:, None, :]
    gnp = jnp.pad(gla_norm_g, ((0, 0), (0, DV_P - DV_C)))[:, None, :]
    wo_c = _pad_head_rows(w_o[:, D_A + D_B:], DV_C, DV_P)
    return w_all, w2p, bgp, gnp, wo_c


def _state_to_kernel(s):
    s = jnp.swapaxes(s, 2, 3)
    return jnp.pad(s, ((0, 0), (0, 0), (0, DV_P - DV_C), (0, DK_P - DK_C)))


def _state_from_kernel(s):
    return jnp.swapaxes(s[:, :, :DV_C, :DK_C], 2, 3)


def _heads_to_rows(a):
    return jnp.transpose(a, (0, 1, 3, 2, 4))


def _set_rows(big, row0, small):
    return lax.dynamic_update_slice(big, small.astype(big.dtype), (row0, 0))


def kernel(x_prompt, x_sample, cache_win_k, cache_win_v, state_pool, state_gla, norm1_g, w_in, w_gate2, b_gate,
           w_pool, pool_scale, gla_norm_g, w_o, norm2_g, w_ffn_gate, w_ffn_up, w_ffn_down, final_norm_g):
    bp, sp, d_model = x_prompt.shape
    bs, ts, _ = x_sample.shape
    depth = w_in.shape[0]
    d_ff = w_ffn_down.shape[1]
    mp, ms = bp * sp, bs * ts
    m_all = mp + ms
    assert min(WIN_MAX, sp) == WIN_MAX
    assert d_ff % 2 == 0

    w_all, w2p, bgp, gnp, wo_c = _relayout_weights(w_in, w_gate2, b_gate, gla_norm_g, w_o)
    cache_k = jnp.transpose(cache_win_k, (0, 1, 3, 2, 4))
    cache_v = jnp.transpose(cache_win_v, (0, 1, 3, 2, 4))
    pool_scale3 = pool_scale[:, None, :]

    x = jnp.concatenate([x_prompt.reshape(mp, d_model), x_sample.reshape(ms, d_model)], axis=0)
    zero_hist = jnp.zeros((bp, HIST_ROWS, D_B), F32)
    zero_state = jnp.zeros((bp, H_C, DV_P, DK_P), F32)
    p_pool, p_gla, s_k, s_v, s_pool, s_gla = [], [], [], [], [], []
    win_k = win_v = None
    xg, ss = _norm_prep(x, norm1_g[0])
    for l in range(depth):
        slabs = _proj(xg, ss, w_all, l)

        out_a, win_k, win_v = _span_attn(slabs, bp, sp, l, depth, win_k, win_v)
        out_b = _pool(slabs, 0, m_all, zero_hist, w_pool, pool_scale3, l, bp, sp, 0)
        out_c, st_p = _gla(slabs, 0, m_all, w2p, bgp, gnp, zero_state, l, bp, sp)
        u_p = jnp.stack([slabs[SLAB_U:SLAB_U + N_POOL, (b + 1) * sp - POOL_BUF:(b + 1) * sp] for b in range(bp)])
        p_pool.append(jnp.transpose(u_p, (0, 2, 1, 3)).reshape(bp, POOL_BUF, D_B))
        p_gla.append(_state_from_kernel(st_p))

        hist = jnp.pad(state_pool[l], ((0, 0), (HIST_ROWS - POOL_BUF, 0), (0, 0)))
        sa = _dec_attn(slabs, mp, cache_k, cache_v, l, bs, ts)
        sb = _pool(slabs, mp, ms, hist, w_pool, pool_scale3, l, bs, ts, PAST_LEN)
        sc, st_s = _gla(slabs, mp, ms, w2p, bgp, gnp, _state_to_kernel(state_gla[l]), l, bs, ts)
        out_a = _set_rows(out_a, mp, sa)
        out_b = _set_rows(out_b, mp, sb)
        out_c = _set_rows(out_c, mp, sc)
        new_kv = slabs[SLAB_K:SLAB_K + 2 * H_A, mp:].reshape(2, H_A, bs, ts, HD_A)
        s_k.append(jnp.swapaxes(new_kv[0], 0, 1))
        s_v.append(jnp.swapaxes(new_kv[1], 0, 1))
        u_s = jnp.transpose(slabs[SLAB_U:SLAB_U + N_POOL, mp:].reshape(N_POOL, bs, ts, LANE), (1, 2, 0, 3))
        u_ext = jnp.concatenate([state_pool[l], u_s.reshape(bs, ts, D_B)], axis=1)
        s_pool.append(u_ext[:, -POOL_BUF:])
        s_gla.append(_state_from_kernel(st_s))

        x, xg, ss = _matmul_res([(w_o, 0, [(out_a, D_A, 0), (out_b, D_B, 0)]),
                                 (wo_c, 0, [(out_c, H_C * DV_P, 0)])], l, x, "out_proj", norm_gain=norm2_g[l])
        f = _glu(xg, ss, w_ffn_gate, w_ffn_up, l)
        half = d_ff // 2
        x = _matmul_res([(w_ffn_down, 0, [(f, half, 0)])], l, x, "ffn_down")
        if l + 1 < depth:
            x, xg, ss = _matmul_res([(w_ffn_down, 1, [(f, half, 1)])], l, x, "ffn_down", norm_gain=norm1_g[l + 1])
        else:
            x = _matmul_res([(w_ffn_down, 1, [(f, half, 1)])], l, x, "ffn_down")

    y_prompt = _rmsnorm(x, final_norm_g, F32, row0=0, rows=mp).reshape(bp, sp, d_model)
    y_sample = _rmsnorm(x, final_norm_g, F32, row0=mp, rows=ms).reshape(bs, ts, d_model)
    return (y_prompt, y_sample, _heads_to_rows(win_k), _heads_to_rows(win_v),
            jnp.stack(p_pool), jnp.stack(p_gla), _heads_to_rows(jnp.stack(s_k)), _heads_to_rows(jnp.stack(s_v)),
            jnp.stack(s_pool), jnp.stack(s_gla))
```

```python
import functools

import jax
import jax.numpy as jnp
from jax import lax
from jax.experimental import pallas as pl
from jax.experimental.pallas import tpu as pltpu

F32 = jnp.float32
BF16 = jnp.bfloat16

H_A, HD_A = 6, 128
D_A = H_A * HD_A
DILATED_PATTERNS = ((128, 1), (512, 4), (2048, 16))
WIN_MAX = 2048
POOL_WINDOWS = (2, 4, 8, 16)
N_POOL = 4
D_B = 512
POOL_GROUP = D_B // N_POOL
POOL_BUF = 15
H_C, DK_C, DV_C = 4, 96, 192
D_C = H_C * DV_C
GATE_RANK = 16
GATE_NORMALIZER = 16.0
GLA_CHUNK = 32
PAST_LEN = 16384
EPS = 1e-6
NEG_INF = -1e30
LOG2E = 1.4426950408889634

LANE = 128
SUBLANE = 8
PACK = 16
VMEM_LIMIT_BYTES = 56 * 1024 * 1024

DK_P = 128
DV_P = 256
HIST_ROWS = 16

VS = DV_P // LANE
SLAB_VC = 0
SLAB_RC = SLAB_VC + H_C * VS
SLAB_QC = SLAB_RC + H_C * VS
SLAB_KC = SLAB_QC + H_C
SLAB_U = SLAB_KC + H_C
SLAB_G = SLAB_U + N_POOL
SLAB_Q = SLAB_G + 2
SLAB_K = SLAB_Q + H_A
SLAB_V = SLAB_K + H_A
N_SLABS = SLAB_V + H_A
D_MIX_P = D_A + D_B + H_C * DV_P

ATTN_BLK = 128
assert all(w // d == ATTN_BLK and d & (d - 1) == 0 for w, d in DILATED_PATTERNS)

TM_ROWS = 1376
TM_NORM = 688
TN_PROJ = 1024
TN_WIDE = 512
TM_RES = 688
TN_RES = 1024
GLA_GROUP = 8


def _params(*sem):
    return pltpu.CompilerParams(dimension_semantics=sem, vmem_limit_bytes=VMEM_LIMIT_BYTES)


def _row_tile(m, target, mult=PACK):
    if m <= target:
        return m
    t = target - target % mult
    while t >= mult:
        if m % t == 0:
            return t
        t -= mult
    raise ValueError(f"no row tile for {m}")


def _rmsnorm_kernel(x_ref, g_ref, o_ref):
    x = x_ref[...]
    ms = jnp.mean(x * x, axis=-1, keepdims=True)
    o_ref[...] = (x * lax.rsqrt(ms + EPS) * g_ref[...]).astype(o_ref.dtype)


def _rmsnorm(x, g, out_dtype, *, row0=0, rows=None):
    d = x.shape[1]
    rows = x.shape[0] if rows is None else rows
    tm = _row_tile(rows, TM_NORM)
    assert row0 % tm == 0
    return pl.pallas_call(
        _rmsnorm_kernel,
        grid=(rows // tm,),
        in_specs=[pl.BlockSpec((tm, d), lambda i: (i + row0 // tm, 0)), pl.BlockSpec((1, d), lambda i: (0, 0))],
        out_specs=pl.BlockSpec((tm, d), lambda i: (i, 0)),
        out_shape=jax.ShapeDtypeStruct((rows, d), out_dtype),
        compiler_params=_params("arbitrary"),
        name="rmsnorm",
    )(x, g.reshape(1, d))


def _inv_rms(ss_ref, rows, width):
    tot = ss_ref[0, rows, :]
    for s in range(1, ss_ref.shape[0]):
        tot = tot + ss_ref[s, rows, :]
    return lax.rsqrt(tot * (1.0 / width) + EPS)


def _norm_prep_kernel(x_ref, g_ref, xg_ref, ss_ref):
    x = x_ref[...]
    xg_ref[...] = (x * g_ref[...]).astype(BF16)
    total = jnp.broadcast_to(jnp.sum(x * x, axis=-1, keepdims=True), (x.shape[0], LANE))
    ss_ref[0] = total
    for s in range(1, ss_ref.shape[0]):
        ss_ref[s] = jnp.zeros_like(total)


def _norm_prep(x, g):
    m, d = x.shape
    tm = _row_tile(m, TM_NORM)
    return pl.pallas_call(
        _norm_prep_kernel,
        grid=(m // tm,),
        in_specs=[pl.BlockSpec((tm, d), lambda i: (i, 0)), pl.BlockSpec((1, d), lambda i: (0, 0))],
        out_specs=[pl.BlockSpec((tm, d), lambda i: (i, 0)), pl.BlockSpec((d // TN_RES, tm, LANE), lambda i: (0, i, 0))],
        out_shape=[jax.ShapeDtypeStruct((m, d), BF16), jax.ShapeDtypeStruct((d // TN_RES, m, LANE), F32)],
        compiler_params=_params("arbitrary"),
        name="norm_prep",
    )(x, g.reshape(1, d))


def _proj_kernel(x_ref, ss_ref, w_ref, o_ref):
    acc = lax.dot_general(x_ref[...], w_ref[...], (((1,), (1,)), ((), ())), preferred_element_type=F32)
    r = _inv_rms(ss_ref, slice(None), x_ref.shape[1])
    for c in range(o_ref.shape[0]):
        o_ref[c] = acc[:, c * LANE:(c + 1) * LANE] * r


def _proj(xg, ss, w_all, layer):
    m, k = xg.shape
    n = w_all.shape[1]
    tm = _row_tile(m, TM_ROWS)
    tn = TN_PROJ
    assert n % tn == 0
    return pl.pallas_call(
        _proj_kernel,
        grid=(m // tm, n // tn),
        in_specs=[pl.BlockSpec((tm, k), lambda i, j: (i, 0)),
                  pl.BlockSpec((ss.shape[0], tm, LANE), lambda i, j: (0, i, 0)),
                  pl.BlockSpec((None, tn, k), lambda i, j: (layer, j, 0))],
        out_specs=pl.BlockSpec((tn // LANE, tm, LANE), lambda i, j: (j, i, 0)),
        out_shape=jax.ShapeDtypeStruct((n // LANE, m, LANE), F32),
        compiler_params=_params("arbitrary", "arbitrary"),
        name="proj_in",
    )(xg, ss, w_all)


def _mm_res_kernel(*refs, x_counts, emit_norm):
    n_x, n_w = sum(x_counts), len(x_counts)
    xs = refs[:n_x]
    ws = refs[n_x:n_x + n_w]
    pos = n_x + n_w
    res_ref = refs[pos]
    gain_ref = refs[pos + 1] if emit_norm else None
    pos += 1 + int(emit_norm)
    o_ref = refs[pos]
    xg_ref, ss_ref = (refs[pos + 1], refs[pos + 2]) if emit_norm else (None, None)
    wbs = refs[pos + 1 + 2 * int(emit_norm):]

    @pl.when(pl.program_id(1) == 0)
    def _():
        for w_ref, wb_ref in zip(ws, wbs):
            wb_ref[...] = w_ref[...].astype(BF16)

    acc = res_ref[...]
    xi = 0
    for wb_ref, count in zip(wbs, x_counts):
        k0 = 0
        for x_ref in xs[xi:xi + count]:
            k = x_ref.shape[1]
            acc = acc + jnp.dot(x_ref[...], wb_ref[k0:k0 + k, :], preferred_element_type=F32)
            k0 += k
        xi += count
    o_ref[...] = acc
    if emit_norm:
        xg_ref[...] = (acc * gain_ref[...]).astype(BF16)
        ss_ref[0] = jnp.broadcast_to(jnp.sum(acc * acc, axis=-1, keepdims=True), (acc.shape[0], LANE))


def _matmul_res(parts, layer, res, name, norm_gain=None):
    m, n = res.shape
    tm = _row_tile(m, TM_RES)
    tn = TN_RES
    assert n % tn == 0
    emit_norm = norm_gain is not None
    x_specs, w_specs, xs, ws, scratch = [], [], [], [], []
    for w, rblk, x_list in parts:
        ktot = sum(width for _, width, _ in x_list)
        for x, width, cb in x_list:
            xs.append(x)
            x_specs.append(pl.BlockSpec((tm, width), lambda j, i, cb=cb: (i, cb)))
        ws.append(w)
        w_specs.append(pl.BlockSpec((None, ktot, tn), lambda j, i, rblk=rblk: (layer, rblk, j)))
        scratch.append(pltpu.VMEM((ktot, tn), BF16))
    tile = pl.BlockSpec((tm, tn), lambda j, i: (i, j))
    in_specs = x_specs + w_specs + [tile]
    args = [*xs, *ws, res]
    out_specs, out_shape = [tile], [jax.ShapeDtypeStruct((m, n), F32)]
    if emit_norm:
        in_specs.append(pl.BlockSpec((1, tn), lambda j, i: (0, j)))
        args.append(norm_gain.reshape(1, n))
        out_specs += [tile, pl.BlockSpec((1, tm, LANE), lambda j, i: (j, i, 0))]
        out_shape += [jax.ShapeDtypeStruct((m, n), BF16), jax.ShapeDtypeStruct((n // tn, m, LANE), F32)]
    out = pl.pallas_call(
        functools.partial(_mm_res_kernel, x_counts=tuple(len(p[2]) for p in parts), emit_norm=emit_norm),
        grid=(n // tn, m // tm),
        in_specs=in_specs,
        out_specs=out_specs,
        out_shape=out_shape,
        scratch_shapes=scratch,
        input_output_aliases={len(xs) + len(ws): 0},
        compiler_params=_params("arbitrary", "arbitrary"),
        name=name,
    )(*args)
    return out if emit_norm else out[0]


def _glu_kernel(x_ref, ss_ref, wg_ref, wu_ref, o_ref, wgb_ref, wub_ref, *, row_split):
    @pl.when(pl.program_id(1) == 0)
    def _():
        wgb_ref[...] = wg_ref[...].astype(BF16)
        wub_ref[...] = wu_ref[...].astype(BF16)

    rows = x_ref.shape[0] // row_split
    reps = o_ref.shape[1] // LANE
    for r in range(row_split):
        rs = slice(r * rows, (r + 1) * rows)
        x = x_ref[rs, :]
        inv = _inv_rms(ss_ref, rs, x_ref.shape[1])
        inv = jnp.concatenate([inv] * reps, axis=-1)
        g = jnp.dot(x, wgb_ref[...], preferred_element_type=F32) * inv
        u = jnp.dot(x, wub_ref[...], preferred_element_type=F32) * inv
        o_ref[rs, :] = (g * (1.0 / (1.0 + jnp.exp(-g))) * u).astype(o_ref.dtype)


def _glu(xg, ss, wg, wu, layer):
    m, k = xg.shape
    n = wg.shape[2]
    tm = _row_tile(m, TM_ROWS)
    tn = TN_WIDE
    row_split = 2 if tm % (2 * PACK) == 0 else 1
    assert n % tn == 0
    w_spec = pl.BlockSpec((None, k, tn), lambda j, i: (layer, 0, j))
    return pl.pallas_call(
        functools.partial(_glu_kernel, row_split=row_split),
        grid=(n // tn, m // tm),
        in_specs=[pl.BlockSpec((tm, k), lambda j, i: (i, 0)),
                  pl.BlockSpec((ss.shape[0], tm, LANE), lambda j, i: (0, i, 0)), w_spec, w_spec],
        out_specs=pl.BlockSpec((tm, tn), lambda j, i: (i, j)),
        out_shape=jax.ShapeDtypeStruct((m, n), BF16),
        scratch_shapes=[pltpu.VMEM((k, tn), BF16), pltpu.VMEM((k, tn), BF16)],
        compiler_params=_params("arbitrary", "arbitrary"),
        name="ffn_glu",
    )(xg, ss, wg, wu)


def _span_attn_kernel(q_ref, k_ref, v_ref, *rest, skew, n_carried):
    o_ref, pk_ref, pv_ref, k_scr, v_scr, o_scr, l_scr, bias_scr = rest[n_carried:]
    n0 = pl.program_id(2)
    span = q_ref.shape[0]
    blk = ATTN_BLK
    units = span // blk
    scale = HD_A ** -0.5
    cur = (n0 % 2) * span
    prev = span - cur

    k_scr[pl.ds(cur, span), :] = k_ref[...]
    v_scr[pl.ds(cur, span), :] = v_ref[...]

    @pl.when(n0 == pl.num_programs(2) - 1)
    def _():
        pk_ref[...] = k_ref[...]
        pv_ref[...] = v_ref[...]

    qi = lax.broadcasted_iota(jnp.int32, (blk, blk), 0)
    ki = lax.broadcasted_iota(jnp.int32, (blk, blk), 1)
    bias_scr[0] = jnp.where(ki <= qi, 0.0, NEG_INF)
    bias_scr[1] = jnp.where(ki >= qi, 0.0, NEG_INF)
    nt = (((1,), (1,)), ((), ()))

    def rows(start, dil):
        return pl.ds(start, blk) if dil == 1 else pl.ds(start, blk, stride=dil)

    def unit_starts(u, dil):
        r, nb = u % dil, u // dil
        start = r + dil * blk * nb
        prev_start = cur + start - dil * blk if nb > 0 else prev + r + span - dil * blk
        return start, prev_start, nb

    def run_units(first_span):
        def scores(u, dil):
            start, prev_start, nb = unit_starts(u, dil)
            q = (q_ref[rows(start, dil), :] * (scale * LOG2E)).astype(BF16)
            kc = k_scr[rows(cur + start, dil), :].astype(BF16)
            sc = lax.dot_general(q, kc, nt, preferred_element_type=F32) + bias_scr[0]
            if first_span and nb == 0:
                return None, sc
            kp = k_scr[rows(prev_start, dil), :].astype(BF16)
            sp = lax.dot_general(q, kp, nt, preferred_element_type=F32) + bias_scr[1]
            return sp, sc

        def finish(p, u, dil, sp, sc):
            start, prev_start, _ = unit_starts(u, dil)
            vc = v_scr[rows(cur + start, dil), :].astype(BF16)
            if sp is None:
                m = jnp.max(sc, axis=-1, keepdims=True)
                pc = jnp.exp2(sc - m)
                den = jnp.sum(pc, axis=-1, keepdims=True)
                o = jnp.dot(pc.astype(BF16), vc, preferred_element_type=F32)
            else:
                vp = v_scr[rows(prev_start, dil), :].astype(BF16)
                m = jnp.max(jnp.maximum(sp, sc), axis=-1, keepdims=True)
                pp = jnp.exp2(sp - m)
                pc = jnp.exp2(sc - m)
                den = jnp.sum(pp + pc, axis=-1, keepdims=True)
                o = jnp.dot(pp.astype(BF16), vp, preferred_element_type=F32)
                o = o + jnp.dot(pc.astype(BF16), vc, preferred_element_type=F32)
            o_scr[p, rows(start, dil), :] = o / den
            l_scr[p, rows(start, dil), :] = jnp.broadcast_to(m + jnp.log(den) * LOG2E, (blk, HD_A))

        order = [(p, u, dil) for p, (_, dil) in enumerate(DILATED_PATTERNS) for u in range(units)]
        pending = []
        for step in range(len(order) + skew):
            if step < len(order):
                p, u, dil = order[step]
                pending.append((p, u, dil) + scores(u, dil))
            if step >= skew:
                finish(*pending.pop(0))

    @pl.when(n0 == 0)
    def _():
        run_units(True)

    @pl.when(n0 > 0)
    def _():
        run_units(False)

    ct = 64

    def combine(c, carry):
        rs = pl.ds(pl.multiple_of(c * ct, ct), ct)
        l1, l2, l3 = l_scr[0, rs, :], l_scr[1, rs, :], l_scr[2, rs, :]
        mx = jnp.maximum(jnp.maximum(l1, l2), l3)
        e1, e2, e3 = jnp.exp2(l1 - mx), jnp.exp2(l2 - mx), jnp.exp2(l3 - mx)
        num = e1 * o_scr[0, rs, :] + e2 * o_scr[1, rs, :] + e3 * o_scr[2, rs, :]
        o_ref[rs, :] = (num / (e1 + e2 + e3)).astype(o_ref.dtype)
        return carry

    lax.fori_loop(0, span // ct, combine, 0, unroll=2)


def _span_attn(slabs, batch, seq, layer, depth, win_k, win_v):
    span = WIN_MAX
    assert seq % span == 0 and len(DILATED_PATTERNS) == 3
    nsp = seq // span
    slab = lambda s0: pl.BlockSpec((None, span, HD_A), lambda b, h, n: (s0 + h, b * nsp + n, 0))
    win_spec = pl.BlockSpec((None, None, None, span, HD_A), lambda b, h, n: (layer, b, h, 0, 0))
    win_shape = jax.ShapeDtypeStruct((depth, batch, H_A, span, HD_A), F32)
    in_specs = [slab(SLAB_Q), slab(SLAB_K), slab(SLAB_V)]
    args = [slabs, slabs, slabs]
    aliases = {}
    if win_k is not None:
        in_specs += [pl.BlockSpec(memory_space=pl.ANY)] * 2
        args += [win_k, win_v]
        aliases = {3: 1, 4: 2}
    skew = 6
    return pl.pallas_call(
        functools.partial(_span_attn_kernel, skew=skew, n_carried=len(aliases)),
        grid=(batch, H_A, nsp),
        in_specs=in_specs,
        out_specs=[pl.BlockSpec((span, HD_A), lambda b, h, n: (b * nsp + n, h)), win_spec, win_spec],
        out_shape=[jax.ShapeDtypeStruct((slabs.shape[1], D_A), BF16), win_shape, win_shape],
        scratch_shapes=[pltpu.VMEM((2 * span, HD_A), F32), pltpu.VMEM((2 * span, HD_A), F32),
                        pltpu.VMEM((3, span, HD_A), F32), pltpu.VMEM((3, span, HD_A), F32),
                        pltpu.VMEM((2, ATTN_BLK, ATTN_BLK), F32)],
        input_output_aliases=aliases,
        compiler_params=_params("arbitrary", "arbitrary", "arbitrary"),
        name="span_attn",
    )(*args)


def _dec_attn_kernel(q_ref, kn_ref, vn_ref, kc_ref, vc_ref, o_ref, *, past):
    t = q_ref.shape[1]
    scale = HD_A ** -0.5
    nt = (((1,), (1,)), ((), ()))
    d_c = past + lax.broadcasted_iota(jnp.int32, (t, past), 0) - lax.broadcasted_iota(jnp.int32, (t, past), 1)
    d_n = lax.broadcasted_iota(jnp.int32, (t, t), 0) - lax.broadcasted_iota(jnp.int32, (t, t), 1)
    masks = []
    for window, dil in DILATED_PATTERNS:
        m_c = jnp.logical_and((d_c & (dil - 1)) == 0, d_c <= window)
        m_n = jnp.logical_and(jnp.logical_and(d_n >= 0, (d_n & (dil - 1)) == 0), d_n <= window)
        masks.append((m_c, m_n))
    for h in range(H_A):
        q = q_ref[h]
        vc = vc_ref[h]
        vn = vn_ref[h]
        s_c = lax.dot_general(q, kc_ref[h], nt, preferred_element_type=F32) * scale
        s_n = lax.dot_general(q, kn_ref[h], nt, preferred_element_type=F32) * scale
        pcs, pns, dens, lses = [], [], [], []
        for m_c, m_n in masks:
            a_c = jnp.where(m_c, s_c, NEG_INF)
            a_n = jnp.where(m_n, s_n, NEG_INF)
            m = jnp.maximum(jnp.max(a_c, axis=-1, keepdims=True), jnp.max(a_n, axis=-1, keepdims=True))
            p_c = jnp.exp(a_c - m)
            p_n = jnp.exp(a_n - m)
            den = jnp.sum(p_c, axis=-1, keepdims=True) + jnp.sum(p_n, axis=-1, keepdims=True)
            pcs.append(p_c)
            pns.append(p_n)
            dens.append(den)
            lses.append(m + jnp.log(den))
        o_all = jnp.dot(jnp.concatenate(pcs, axis=0), vc, preferred_element_type=F32)
        o_all = o_all + jnp.dot(jnp.concatenate(pns, axis=0), vn, preferred_element_type=F32)
        outs = [o_all[i * t:(i + 1) * t] / dens[i] for i in range(len(masks))]
        mx = jnp.maximum(jnp.maximum(lses[0], lses[1]), lses[2])
        es = [jnp.exp(l - mx) for l in lses]
        num = es[0] * outs[0] + es[1] * outs[1] + es[2] * outs[2]
        o_ref[:, h * HD_A:(h + 1) * HD_A] = (num / (es[0] + es[1] + es[2])).astype(o_ref.dtype)


def _dec_attn(slabs, row0, cache_k, cache_v, layer, batch, t):
    past = cache_k.shape[3]
    assert row0 % t == 0 and SLAB_Q % H_A == 0 and SLAB_K % H_A == 0 and SLAB_V % H_A == 0
    new_spec = lambda s0: pl.BlockSpec((H_A, t, HD_A), lambda b: (s0 // H_A, row0 // t + b, 0))
    cache_spec = pl.BlockSpec((None, None, H_A, past, HD_A), lambda b: (layer, b, 0, 0, 0))
    return pl.pallas_call(
        functools.partial(_dec_attn_kernel, past=past),
        grid=(batch,),
        in_specs=[new_spec(SLAB_Q), new_spec(SLAB_K), new_spec(SLAB_V), cache_spec, cache_spec],
        out_specs=pl.BlockSpec((t, D_A), lambda b: (b, 0)),
        out_shape=jax.ShapeDtypeStruct((batch * t, D_A), F32),
        compiler_params=_params("arbitrary"),
        name="dec_attn",
    )(slabs, slabs, slabs, cache_k, cache_v)


def _pool_kernel(*refs, pos0, has_prev):
    if has_prev:
        u_ref, prev_ref, hist_ref, w_ref, sc_ref, o_ref, ext_ref = refs
    else:
        u_ref, hist_ref, w_ref, sc_ref, o_ref, ext_ref = refs
        prev_ref = None
    i = pl.program_id(1)
    tm = u_ref.shape[1]

    @pl.when(i == 0)
    def _():
        for gi in range(N_POOL):
            ext_ref[gi, 0:HIST_ROWS, :] = hist_ref[0, :, gi * POOL_GROUP:(gi + 1) * POOL_GROUP]

    if has_prev:
        @pl.when(i > 0)
        def _():
            ext_ref[:, 0:HIST_ROWS, :] = prev_ref[...]

    ext_ref[:, HIST_ROWS:HIST_ROWS + tm, :] = u_ref[...]
    pos = pos0 + i * tm + lax.broadcasted_iota(jnp.int32, (tm, POOL_GROUP), 0)
    for gi, w in enumerate(POOL_WINDOWS):
        cs = slice(gi * POOL_GROUP, (gi + 1) * POOL_GROUP)
        tok = ext_ref[gi, HIST_ROWS:HIST_ROWS + tm, :]
        tot = tok
        for back in range(1, w):
            tot = tot + ext_ref[gi, HIST_ROWS - back:HIST_ROWS - back + tm, :]
        cnt = jnp.minimum(w, pos + 1).astype(F32)
        pooled = tot / cnt - tok
        out = jnp.dot(pooled.astype(BF16), w_ref[gi].astype(BF16), preferred_element_type=F32)
        o_ref[:, cs] = (out * sc_ref[:, cs]).astype(o_ref.dtype)


def _pool(slabs, row0, out_rows, hist, w_pool, scale, layer, batch, t, pos0):
    tm = _row_tile(t, 512)
    nt = t // tm
    has_prev = nt > 1
    assert row0 % tm == 0
    rb = lambda b, i: row0 // tm + b * nt + i
    ob = lambda b, i: b * nt + i
    in_specs = [pl.BlockSpec((N_POOL, tm, LANE), lambda b, i: (SLAB_U // N_POOL, rb(b, i), 0))]
    args = [slabs]
    if has_prev:
        per = tm // HIST_ROWS
        in_specs.append(pl.BlockSpec((N_POOL, HIST_ROWS, LANE),
                                     lambda b, i: (SLAB_U // N_POOL, jnp.maximum(rb(b, i) * per - 1, 0), 0)))
        args.append(slabs)
    in_specs += [
        pl.BlockSpec((1, HIST_ROWS, D_B), lambda b, i: (b, 0, 0)),
        pl.BlockSpec((None, N_POOL, POOL_GROUP, POOL_GROUP), lambda b, i: (layer, 0, 0, 0)),
        pl.BlockSpec((None, 1, D_B), lambda b, i: (layer, 0, 0)),
    ]
    args += [hist, w_pool, scale]
    return pl.pallas_call(
        functools.partial(_pool_kernel, pos0=pos0, has_prev=has_prev),
        grid=(batch, nt),
        in_specs=in_specs,
        out_specs=pl.BlockSpec((tm, D_B), lambda b, i: (ob(b, i), 0)),
        out_shape=jax.ShapeDtypeStruct((out_rows, D_B), BF16 if tm % PACK == 0 else F32),
        scratch_shapes=[pltpu.VMEM((N_POOL, HIST_ROWS + tm, LANE), F32)],
        compiler_params=_params("arbitrary", "arbitrary"),
        name="pool_mix",
    )(*args)


def _gla_kernel(q_ref, k_ref, v_ref, r_ref, ga_ref, w2_ref, bg_ref, gn_ref, s0_ref, o_ref, s_out_ref,
                st_ref, g_ref, *, chunk, group):
    i = pl.program_id(1)
    ts = q_ref.shape[1]

    @pl.when(i == 0)
    def _():
        st_ref[...] = s0_ref[0]

    z = jnp.dot(ga_ref[...], w2_ref[...], preferred_element_type=F32) + bg_ref[...]
    g_ref[...] = (jnp.minimum(z, 0.0) - jnp.log(1.0 + jnp.exp(-jnp.abs(z)))) / GATE_NORMALIZER

    gr = group * chunk
    ri = lax.broadcasted_iota(jnp.int32, (gr, gr), 0)
    ci = lax.broadcasted_iota(jnp.int32, (gr, gr), 1)
    same_chunk = (ri // chunk) == (ci // chunk)
    tril = jnp.logical_and(same_chunk, ci <= ri)
    sum_mat = jnp.concatenate([tril.astype(F32), same_chunk.astype(F32)], axis=0)
    split_sums = gr % PACK == 0
    qscale = DK_C ** -0.5
    nt = (((1,), (1,)), ((), ()))
    tn = (((0,), (0,)), ((), ()))

    def body(gi, carry):
        rows = pl.ds(pl.multiple_of(gi * gr, gr), gr)
        if split_sums:
            rem = g_ref[rows, :]
            sums = None
            for _ in range(3):
                part = rem.astype(BF16)
                rem = rem - part.astype(F32)
                d = jnp.dot(sum_mat.astype(BF16), part, preferred_element_type=F32)
                sums = d if sums is None else sums + d
        else:
            sums = jnp.dot(sum_mat, g_ref[rows, :], preferred_element_type=F32, precision=lax.Precision.HIGHEST)
        b = sums[:gr]
        bl = sums[gr:]
        eb = jnp.exp(b)
        enb = jnp.exp(-b)
        ebl = jnp.exp(bl)
        ekd = jnp.exp(bl - b)
        heads = range(H_C)
        ksl = [slice(h * DK_P, (h + 1) * DK_P) for h in heads]
        vh = [jnp.concatenate([v_ref[VS * h + s, rows, :] for s in range(VS)], axis=-1).astype(BF16) for h in heads]
        qe = [(q_ref[h, rows, :] * qscale * eb[:, ksl[h]]).astype(BF16) for h in heads]
        ke = [(k_ref[h, rows, :] * enb[:, ksl[h]]).astype(BF16) for h in heads]
        kd = [(k_ref[h, rows, :] * ekd[:, ksl[h]]).astype(BF16) for h in heads]
        a = [lax.dot_general(qe[h], ke[h], nt, preferred_element_type=F32) for h in heads]
        st = [st_ref[h] for h in heads]
        inter = [[] for _ in heads]
        for c in range(group):
            cr = slice(c * chunk, (c + 1) * chunk)
            kv = [lax.dot_general(vh[h][cr], kd[h][cr], tn, preferred_element_type=F32) for h in heads]
            for h in heads:
                inter[h].append(lax.dot_general(qe[h][cr], st[h].astype(BF16), nt,
                                                preferred_element_type=F32))
            st = [st[h] * ebl[c * chunk:c * chunk + 1, ksl[h]] + kv[h] for h in heads]
        for h in heads:
            st_ref[h] = st[h]
            am = jnp.where(tril, a[h], 0.0).astype(BF16)
            o = jnp.dot(am, vh[h], preferred_element_type=F32)
            o = o + (inter[h][0] if group == 1 else jnp.concatenate(inter[h], axis=0))
            ms = jnp.sum(o * o, axis=-1, keepdims=True) * (1.0 / DV_C)
            on = o * lax.rsqrt(ms + EPS) * gn_ref[...]
            rh = jnp.concatenate([r_ref[VS * h + s, rows, :] for s in range(VS)], axis=-1)
            o_ref[rows, h * DV_P:(h + 1) * DV_P] = (on * (rh * (1.0 / (1.0 + jnp.exp(-rh))))).astype(o_ref.dtype)
        return carry

    lax.fori_loop(0, ts // gr, body, 0, unroll=2)

    @pl.when(i == pl.num_programs(1) - 1)
    def _():
        s_out_ref[0] = st_ref[...]


def _gla(slabs, row0, out_rows, w2p, bgp, gnp, s0t, layer, batch, t):
    chunk = min(GLA_CHUNK, t)
    out_dtype = BF16 if chunk % PACK == 0 else F32
    ts = _row_tile(t, 512)
    nt = t // ts
    group = GLA_GROUP if ts % (GLA_GROUP * chunk) == 0 else 1
    assert t % chunk == 0 and ts % chunk == 0 and row0 % ts == 0
    qk_w, v_w = H_C * DK_P, H_C * DV_P
    rb = lambda b, i: row0 // ts + b * nt + i
    grp = lambda n, s0: pl.BlockSpec((n, ts, LANE), lambda b, i: (s0 // n, rb(b, i), 0))
    assert SLAB_VC % (H_C * VS) == 0 and SLAB_RC % (H_C * VS) == 0 and SLAB_QC % H_C == 0 and SLAB_KC % H_C == 0
    par = lambda shape: pl.BlockSpec((None,) + shape, lambda b, i: (layer,) + (0,) * len(shape))
    state_spec = pl.BlockSpec((1, H_C, DV_P, DK_P), lambda b, i: (b, 0, 0, 0))
    return pl.pallas_call(
        functools.partial(_gla_kernel, chunk=chunk, group=group),
        grid=(batch, nt),
        in_specs=[grp(H_C, SLAB_QC), grp(H_C, SLAB_KC), grp(H_C * VS, SLAB_VC), grp(H_C * VS, SLAB_RC),
                  pl.BlockSpec((None, ts, LANE), lambda b, i: (SLAB_G, rb(b, i), 0)),
                  par((LANE, qk_w)), par((1, qk_w)), par((1, DV_P)), state_spec],
        out_specs=[pl.BlockSpec((ts, v_w), lambda b, i: (b * nt + i, 0)), state_spec],
        out_shape=[jax.ShapeDtypeStruct((out_rows, v_w), out_dtype),
                   jax.ShapeDtypeStruct((batch, H_C, DV_P, DK_P), F32)],
        scratch_shapes=[pltpu.VMEM((H_C, DV_P, DK_P), F32), pltpu.VMEM((ts, qk_w), F32)],
        compiler_params=_params("arbitrary", "arbitrary"),
        name="gla",
    )(slabs, slabs, slabs, slabs, slabs, w2p, bgp, gnp, s0t)


def _pad_head_rows(w, width, padded):
    depth, _, k = w.shape
    w = w.reshape(depth, H_C, width, k)
    w = jnp.pad(w, ((0, 0), (0, 0), (0, padded - width), (0, 0)))
    return w.reshape(depth, H_C * padded, k)


def _pad_heads(w, width, padded):
    lead = w.shape[:-1]
    w = w.reshape(lead + (H_C, width))
    w = jnp.pad(w, [(0, 0)] * len(lead) + [(0, 0), (0, padded - width)])
    return w.reshape(lead + (H_C * padded,))


def _relayout_weights(w_in, w_gate2, b_gate, gla_norm_g, w_o):
    wt = jnp.swapaxes(w_in, 1, 2)
    c0 = 3 * D_A + D_B
    qk = H_C * DK_C
    wq = wt[:, c0:c0 + qk]
    wk = wt[:, c0 + qk:c0 + 2 * qk]
    wv = wt[:, c0 + 2 * qk:c0 + 2 * qk + D_C]
    wr = wt[:, c0 + 2 * qk + D_C:c0 + 2 * qk + 2 * D_C]
    wg = wt[:, c0 + 2 * qk + 2 * D_C:]
    w_all = jnp.concatenate([
        _pad_head_rows(wv, DV_C, DV_P), _pad_head_rows(wr, DV_C, DV_P),
        _pad_head_rows(wq, DK_C, DK_P), _pad_head_rows(wk, DK_C, DK_P),
        wt[:, 3 * D_A:c0],
        jnp.pad(wg, ((0, 0), (0, 2 * LANE - GATE_RANK), (0, 0))),
        wt[:, :3 * D_A],
    ], axis=1).astype(BF16)
    assert w_all.shape[1] == N_SLABS * LANE
    w2p = jnp.pad(_pad_heads(w_gate2, DK_C, DK_P), ((0, 0), (0, LANE - GATE_RANK), (0, 0)))
    bgp = _pad_heads(b_gate, DK_C, DK_P)[:, None, :]
    gnp = jnp.pad(gla_norm_g, ((0, 0), (0, DV_P - DV_C)))[:, None, :]
    wo_c = _pad_head_rows(w_o[:, D_A + D_B:], DV_C, DV_P)
    return w_all, w2p, bgp, gnp, wo_c


def _state_to_kernel(s):
    s = jnp.swapaxes(s, -1, -2)
    return jnp.pad(s, [(0, 0)] * (s.ndim - 2) + [(0, DV_P - DV_C), (0, DK_P - DK_C)])


def _state_from_kernel(s):
    return jnp.swapaxes(s[..., :DV_C, :DK_C], -1, -2)


def _heads_to_rows(a):
    return jnp.transpose(a, (0, 1, 3, 2, 4))


def _set_rows(big, row0, small):
    return lax.dynamic_update_slice(big, small.astype(big.dtype), (row0, 0))


def kernel(x_prompt, x_sample, cache_win_k, cache_win_v, state_pool, state_gla, norm1_g, w_in, w_gate2, b_gate,
           w_pool, pool_scale, gla_norm_g, w_o, norm2_g, w_ffn_gate, w_ffn_up, w_ffn_down, final_norm_g):
    bp, sp, d_model = x_prompt.shape
    bs, ts, _ = x_sample.shape
    depth = w_in.shape[0]
    d_ff = w_ffn_down.shape[1]
    mp, ms = bp * sp, bs * ts
    m_all = mp + ms
    assert min(WIN_MAX, sp) == WIN_MAX
    assert d_ff % 2 == 0

    w_all, w2p, bgp, gnp, wo_c = _relayout_weights(w_in, w_gate2, b_gate, gla_norm_g, w_o)
    cache_k = jnp.transpose(cache_win_k, (0, 1, 3, 2, 4))
    cache_v = jnp.transpose(cache_win_v, (0, 1, 3, 2, 4))
    pool_scale3 = pool_scale[:, None, :]

    x = jnp.concatenate([x_prompt.reshape(mp, d_model), x_sample.reshape(ms, d_model)], axis=0)
    zero_hist = jnp.zeros((bp, HIST_ROWS, D_B), F32)
    zero_state = jnp.zeros((bp, H_C, DV_P, DK_P), F32)
    hist_all = jnp.pad(state_pool, ((0, 0), (0, 0), (HIST_ROWS - POOL_BUF, 0), (0, 0)))
    state_all = _state_to_kernel(state_gla)
    p_pool, p_gla, s_k, s_v, s_pool, s_gla = [], [], [], [], [], []
    win_k = win_v = None
    xg, ss = _norm_prep(x, norm1_g[0])
    for l in range(depth):
        slabs = _proj(xg, ss, w_all, l)

        out_a, win_k, win_v = _span_attn(slabs, bp, sp, l, depth, win_k, win_v)
        out_b = _pool(slabs, 0, m_all, zero_hist, w_pool, pool_scale3, l, bp, sp, 0)
        out_c, st_p = _gla(slabs, 0, m_all, w2p, bgp, gnp, zero_state, l, bp, sp)
        u_p = jnp.stack([slabs[SLAB_U:SLAB_U + N_POOL, (b + 1) * sp - POOL_BUF:(b + 1) * sp] for b in range(bp)])
        p_pool.append(jnp.transpose(u_p, (0, 2, 1, 3)).reshape(bp, POOL_BUF, D_B))
        p_gla.append(st_p)

        sa = _dec_attn(slabs, mp, cache_k, cache_v, l, bs, ts)
        sb = _pool(slabs, mp, ms, hist_all[l], w_pool, pool_scale3, l, bs, ts, PAST_LEN)
        sc, st_s = _gla(slabs, mp, ms, w2p, bgp, gnp, state_all[l], l, bs, ts)
        out_a = _set_rows(out_a, mp, sa)
        out_b = _set_rows(out_b, mp, sb)
        out_c = _set_rows(out_c, mp, sc)
        new_kv = slabs[SLAB_K:SLAB_K + 2 * H_A, mp:].reshape(2, H_A, bs, ts, HD_A)
        s_k.append(jnp.swapaxes(new_kv[0], 0, 1))
        s_v.append(jnp.swapaxes(new_kv[1], 0, 1))
        u_s = jnp.transpose(slabs[SLAB_U:SLAB_U + N_POOL, mp:].reshape(N_POOL, bs, ts, LANE), (1, 2, 0, 3))
        u_ext = jnp.concatenate([state_pool[l], u_s.reshape(bs, ts, D_B)], axis=1)
        s_pool.append(u_ext[:, -POOL_BUF:])
        s_gla.append(st_s)

        x, xg, ss = _matmul_res([(w_o, 0, [(out_a, D_A, 0), (out_b, D_B, 0)]),
                                 (wo_c, 0, [(out_c, H_C * DV_P, 0)])], l, x, "out_proj", norm_gain=norm2_g[l])
        f = _glu(xg, ss, w_ffn_gate, w_ffn_up, l)
        half = d_ff // 2
        x = _matmul_res([(w_ffn_down, 0, [(f, half, 0)])], l, x, "ffn_down")
        if l + 1 < depth:
            x, xg, ss = _matmul_res([(w_ffn_down, 1, [(f, half, 1)])], l, x, "ffn_down", norm_gain=norm1_g[l + 1])
        else:
            x = _matmul_res([(w_ffn_down, 1, [(f, half, 1)])], l, x, "ffn_down")

    y_prompt = _rmsnorm(x, final_norm_g, F32, row0=0, rows=mp).reshape(bp, sp, d_model)
    y_sample = _rmsnorm(x, final_norm_g, F32, row0=mp, rows=ms).reshape(bs, ts, d_model)
    return (y_prompt, y_sample, _heads_to_rows(win_k), _heads_to_rows(win_v),
            jnp.stack(p_pool), _state_from_kernel(jnp.stack(p_gla)), _heads_to_rows(jnp.stack(s_k)),
            _heads_to_rows(jnp.stack(s_v)), jnp.stack(s_pool), _state_from_kernel(jnp.stack(s_gla)))
```

```python
import functools

import jax
import jax.numpy as jnp
from jax import lax
from jax.experimental import pallas as pl
from jax.experimental.pallas import tpu as pltpu

F32 = jnp.float32
BF16 = jnp.bfloat16

H_A, HD_A = 6, 128
D_A = H_A * HD_A
DILATED_PATTERNS = ((128, 1), (512, 4), (2048, 16))
WIN_MAX = 2048
POOL_WINDOWS = (2, 4, 8, 16)
N_POOL = 4
D_B = 512
POOL_GROUP = D_B // N_POOL
POOL_BUF = 15
H_C, DK_C, DV_C = 4, 96, 192
D_C = H_C * DV_C
GATE_RANK = 16
GATE_NORMALIZER = 16.0
GLA_CHUNK = 32
PAST_LEN = 16384
EPS = 1e-6
NEG_INF = -1e30
LOG2E = 1.4426950408889634

LANE = 128
SUBLANE = 8
PACK = 16
VMEM_LIMIT_BYTES = 56 * 1024 * 1024

DK_P = 128
DV_P = 256
HIST_ROWS = 16

VS = DV_P // LANE
SLAB_VC = 0
SLAB_RC = SLAB_VC + H_C * VS
SLAB_QC = SLAB_RC + H_C * VS
SLAB_KC = SLAB_QC + H_C
SLAB_U = SLAB_KC + H_C
SLAB_G = SLAB_U + N_POOL
SLAB_Q = SLAB_G + 2
SLAB_K = SLAB_Q + H_A
SLAB_V = SLAB_K + H_A
N_SLABS = SLAB_V + H_A
D_MIX_P = D_A + D_B + H_C * DV_P

ATTN_BLK = 128
assert all(w // d == ATTN_BLK and d & (d - 1) == 0 for w, d in DILATED_PATTERNS)

TM_ROWS = 1376
TM_NORM = 1024
TN_PROJ = 1024
TN_WIDE = 512
TM_RES = 688
TN_RES = 1024
GLA_GROUP = 8


def _params(*sem):
    return pltpu.CompilerParams(dimension_semantics=sem, vmem_limit_bytes=VMEM_LIMIT_BYTES)


def _row_tile(m, target, mult=PACK):
    if m <= target:
        return m
    t = target - target % mult
    while t >= mult:
        if m % t == 0:
            return t
        t -= mult
    raise ValueError(f"no row tile for {m}")


def _rmsnorm_kernel(x_ref, g_ref, o_ref):
    x = x_ref[...]
    ms = jnp.mean(x * x, axis=-1, keepdims=True)
    o_ref[...] = (x * lax.rsqrt(ms + EPS) * g_ref[...]).astype(o_ref.dtype)


def _rmsnorm(x, g, out_dtype, *, row0=0, rows=None):
    d = x.shape[1]
    rows = x.shape[0] if rows is None else rows
    tm = _row_tile(rows, TM_NORM)
    assert row0 % tm == 0
    return pl.pallas_call(
        _rmsnorm_kernel,
        grid=(rows // tm,),
        in_specs=[pl.BlockSpec((tm, d), lambda i: (i + row0 // tm, 0)), pl.BlockSpec((1, d), lambda i: (0, 0))],
        out_specs=pl.BlockSpec((tm, d), lambda i: (i, 0)),
        out_shape=jax.ShapeDtypeStruct((rows, d), out_dtype),
        compiler_params=_params("arbitrary"),
        name="rmsnorm",
    )(x, g.reshape(1, d))


def _inv_rms(ss_ref, rows, width):
    tot = ss_ref[0, rows, :]
    for s in range(1, ss_ref.shape[0]):
        tot = tot + ss_ref[s, rows, :]
    return lax.rsqrt(tot * (1.0 / width) + EPS)


def _norm_prep_kernel(x_ref, g_ref, xg_ref, ss_ref):
    x = x_ref[...]
    xg_ref[...] = (x * g_ref[...]).astype(BF16)
    total = jnp.broadcast_to(jnp.sum(x * x, axis=-1, keepdims=True), (x.shape[0], LANE))
    ss_ref[0] = total
    for s in range(1, ss_ref.shape[0]):
        ss_ref[s] = jnp.zeros_like(total)


def _norm_prep(x, g):
    m, d = x.shape
    tm = _row_tile(m, TM_NORM)
    return pl.pallas_call(
        _norm_prep_kernel,
        grid=(m // tm,),
        in_specs=[pl.BlockSpec((tm, d), lambda i: (i, 0)), pl.BlockSpec((1, d), lambda i: (0, 0))],
        out_specs=[pl.BlockSpec((tm, d), lambda i: (i, 0)), pl.BlockSpec((d // TN_RES, tm, LANE), lambda i: (0, i, 0))],
        out_shape=[jax.ShapeDtypeStruct((m, d), BF16), jax.ShapeDtypeStruct((d // TN_RES, m, LANE), F32)],
        compiler_params=_params("arbitrary"),
        name="norm_prep",
    )(x, g.reshape(1, d))


def _proj_kernel(x_ref, ss_ref, w_ref, o_ref):
    acc = lax.dot_general(x_ref[...], w_ref[...], (((1,), (1,)), ((), ())), preferred_element_type=F32)
    r = _inv_rms(ss_ref, slice(None), x_ref.shape[1])
    for c in range(o_ref.shape[0]):
        o_ref[c] = acc[:, c * LANE:(c + 1) * LANE] * r


def _proj(xg, ss, w_all, layer):
    m, k = xg.shape
    n = w_all.shape[1]
    tm = _row_tile(m, TM_ROWS)
    tn = TN_PROJ
    assert n % tn == 0
    return pl.pallas_call(
        _proj_kernel,
        grid=(m // tm, n // tn),
        in_specs=[pl.BlockSpec((tm, k), lambda i, j: (i, 0)),
                  pl.BlockSpec((ss.shape[0], tm, LANE), lambda i, j: (0, i, 0)),
                  pl.BlockSpec((None, tn, k), lambda i, j: (layer, j, 0))],
        out_specs=pl.BlockSpec((tn // LANE, tm, LANE), lambda i, j: (j, i, 0)),
        out_shape=jax.ShapeDtypeStruct((n // LANE, m, LANE), F32),
        compiler_params=_params("arbitrary", "arbitrary"),
        name="proj_in",
    )(xg, ss, w_all)


def _mm_res_kernel(*refs, x_counts, emit_norm):
    n_x, n_w = sum(x_counts), len(x_counts)
    xs = refs[:n_x]
    ws = refs[n_x:n_x + n_w]
    pos = n_x + n_w
    res_ref = refs[pos]
    gain_ref = refs[pos + 1] if emit_norm else None
    pos += 1 + int(emit_norm)
    o_ref = refs[pos]
    xg_ref, ss_ref = (refs[pos + 1], refs[pos + 2]) if emit_norm else (None, None)
    wbs = refs[pos + 1 + 2 * int(emit_norm):]

    @pl.when(pl.program_id(1) == 0)
    def _():
        for w_ref, wb_ref in zip(ws, wbs):
            wb_ref[...] = w_ref[...].astype(BF16)

    acc = res_ref[...]
    xi = 0
    for wb_ref, count in zip(wbs, x_counts):
        k0 = 0
        for x_ref in xs[xi:xi + count]:
            k = x_ref.shape[1]
            acc = acc + jnp.dot(x_ref[...], wb_ref[k0:k0 + k, :], preferred_element_type=F32)
            k0 += k
        xi += count
    o_ref[...] = acc
    if emit_norm:
        xg_ref[...] = (acc * gain_ref[...]).astype(BF16)
        ss_ref[0] = jnp.broadcast_to(jnp.sum(acc * acc, axis=-1, keepdims=True), (acc.shape[0], LANE))


def _matmul_res(parts, layer, res, name, norm_gain=None):
    m, n = res.shape
    tm = _row_tile(m, TM_RES)
    tn = TN_RES
    assert n % tn == 0
    emit_norm = norm_gain is not None
    x_specs, w_specs, xs, ws, scratch = [], [], [], [], []
    for w, rblk, x_list in parts:
        ktot = sum(width for _, width, _ in x_list)
        for x, width, cb in x_list:
            xs.append(x)
            x_specs.append(pl.BlockSpec((tm, width), lambda j, i, cb=cb: (i, cb)))
        ws.append(w)
        w_specs.append(pl.BlockSpec((None, ktot, tn), lambda j, i, rblk=rblk: (layer, rblk, j)))
        scratch.append(pltpu.VMEM((ktot, tn), BF16))
    tile = pl.BlockSpec((tm, tn), lambda j, i: (i, j))
    in_specs = x_specs + w_specs + [tile]
    args = [*xs, *ws, res]
    out_specs, out_shape = [tile], [jax.ShapeDtypeStruct((m, n), F32)]
    if emit_norm:
        in_specs.append(pl.BlockSpec((1, tn), lambda j, i: (0, j)))
        args.append(norm_gain.reshape(1, n))
        out_specs += [tile, pl.BlockSpec((1, tm, LANE), lambda j, i: (j, i, 0))]
        out_shape += [jax.ShapeDtypeStruct((m, n), BF16), jax.ShapeDtypeStruct((n // tn, m, LANE), F32)]
    out = pl.pallas_call(
        functools.partial(_mm_res_kernel, x_counts=tuple(len(p[2]) for p in parts), emit_norm=emit_norm),
        grid=(n // tn, m // tm),
        in_specs=in_specs,
        out_specs=out_specs,
        out_shape=out_shape,
        scratch_shapes=scratch,
        input_output_aliases={len(xs) + len(ws): 0},
        compiler_params=_params("arbitrary", "arbitrary"),
        name=name,
    )(*args)
    return out if emit_norm else out[0]


def _glu_kernel(x_ref, ss_ref, wg_ref, wu_ref, o_ref, wgb_ref, wub_ref, *, row_split):
    @pl.when(pl.program_id(1) == 0)
    def _():
        wgb_ref[...] = wg_ref[...].astype(BF16)
        wub_ref[...] = wu_ref[...].astype(BF16)

    rows = x_ref.shape[0] // row_split
    reps = o_ref.shape[1] // LANE
    for r in range(row_split):
        rs = slice(r * rows, (r + 1) * rows)
        x = x_ref[rs, :]
        inv = _inv_rms(ss_ref, rs, x_ref.shape[1])
        inv = jnp.concatenate([inv] * reps, axis=-1)
        g = jnp.dot(x, wgb_ref[...], preferred_element_type=F32) * inv
        u = jnp.dot(x, wub_ref[...], preferred_element_type=F32) * inv
        o_ref[rs, :] = (g * (1.0 / (1.0 + jnp.exp(-g))) * u).astype(o_ref.dtype)


def _glu(xg, ss, wg, wu, layer):
    m, k = xg.shape
    n = wg.shape[2]
    tm = _row_tile(m, TM_ROWS)
    tn = TN_WIDE
    row_split = 2 if tm % (2 * PACK) == 0 else 1
    assert n % tn == 0
    w_spec = pl.BlockSpec((None, k, tn), lambda j, i: (layer, 0, j))
    return pl.pallas_call(
        functools.partial(_glu_kernel, row_split=row_split),
        grid=(n // tn, m // tm),
        in_specs=[pl.BlockSpec((tm, k), lambda j, i: (i, 0)),
                  pl.BlockSpec((ss.shape[0], tm, LANE), lambda j, i: (0, i, 0)), w_spec, w_spec],
        out_specs=pl.BlockSpec((tm, tn), lambda j, i: (i, j)),
        out_shape=jax.ShapeDtypeStruct((m, n), BF16),
        scratch_shapes=[pltpu.VMEM((k, tn), BF16), pltpu.VMEM((k, tn), BF16)],
        compiler_params=_params("arbitrary", "arbitrary"),
        name="ffn_glu",
    )(xg, ss, wg, wu)


def _span_attn_kernel(q_ref, k_ref, v_ref, *rest, skew, n_carried):
    o_ref, pk_ref, pv_ref, k_scr, v_scr, o_scr, l_scr, bias_scr = rest[n_carried:]
    n0 = pl.program_id(2)
    span = q_ref.shape[0]
    blk = ATTN_BLK
    units = span // blk
    scale = HD_A ** -0.5
    cur = (n0 % 2) * span
    prev = span - cur

    k_scr[pl.ds(cur, span), :] = k_ref[...]
    v_scr[pl.ds(cur, span), :] = v_ref[...]

    @pl.when(n0 == pl.num_programs(2) - 1)
    def _():
        pk_ref[...] = k_ref[...]
        pv_ref[...] = v_ref[...]

    qi = lax.broadcasted_iota(jnp.int32, (blk, blk), 0)
    ki = lax.broadcasted_iota(jnp.int32, (blk, blk), 1)
    bias_scr[0] = jnp.where(ki <= qi, 0.0, NEG_INF)
    bias_scr[1] = jnp.where(ki >= qi, 0.0, NEG_INF)
    nt = (((1,), (1,)), ((), ()))

    def rows(start, dil):
        return pl.ds(start, blk) if dil == 1 else pl.ds(start, blk, stride=dil)

    def unit_starts(u, dil):
        r, nb = u % dil, u // dil
        start = r + dil * blk * nb
        prev_start = cur + start - dil * blk if nb > 0 else prev + r + span - dil * blk
        return start, prev_start, nb

    def run_units(first_span):
        def scores(u, dil):
            start, prev_start, nb = unit_starts(u, dil)
            q = (q_ref[rows(start, dil), :] * (scale * LOG2E)).astype(BF16)
            kc = k_scr[rows(cur + start, dil), :].astype(BF16)
            sc = lax.dot_general(q, kc, nt, preferred_element_type=F32) + bias_scr[0]
            if first_span and nb == 0:
                return None, sc
            kp = k_scr[rows(prev_start, dil), :].astype(BF16)
            sp = lax.dot_general(q, kp, nt, preferred_element_type=F32) + bias_scr[1]
            return sp, sc

        def finish(p, u, dil, sp, sc):
            start, prev_start, _ = unit_starts(u, dil)
            vc = v_scr[rows(cur + start, dil), :].astype(BF16)
            if sp is None:
                m = jnp.max(sc, axis=-1, keepdims=True)
                pc = jnp.exp2(sc - m)
                den = jnp.sum(pc, axis=-1, keepdims=True)
                o = jnp.dot(pc.astype(BF16), vc, preferred_element_type=F32)
            else:
                vp = v_scr[rows(prev_start, dil), :].astype(BF16)
                m = jnp.max(jnp.maximum(sp, sc), axis=-1, keepdims=True)
                pp = jnp.exp2(sp - m)
                pc = jnp.exp2(sc - m)
                den = jnp.sum(pp + pc, axis=-1, keepdims=True)
                o = jnp.dot(pp.astype(BF16), vp, preferred_element_type=F32)
                o = o + jnp.dot(pc.astype(BF16), vc, preferred_element_type=F32)
            o_scr[p, rows(start, dil), :] = o / den
            l_scr[p, rows(start, dil), :] = jnp.broadcast_to(m + jnp.log(den) * LOG2E, (blk, HD_A))

        order = [(p, u, dil) for p, (_, dil) in enumerate(DILATED_PATTERNS) for u in range(units)]
        pending = []
        for step in range(len(order) + skew):
            if step < len(order):
                p, u, dil = order[step]
                pending.append((p, u, dil) + scores(u, dil))
            if step >= skew:
                finish(*pending.pop(0))

    @pl.when(n0 == 0)
    def _():
        run_units(True)

    @pl.when(n0 > 0)
    def _():
        run_units(False)

    ct = 64

    def combine(c, carry):
        rs = pl.ds(pl.multiple_of(c * ct, ct), ct)
        l1, l2, l3 = l_scr[0, rs, :], l_scr[1, rs, :], l_scr[2, rs, :]
        mx = jnp.maximum(jnp.maximum(l1, l2), l3)
        e1, e2, e3 = jnp.exp2(l1 - mx), jnp.exp2(l2 - mx), jnp.exp2(l3 - mx)
        num = e1 * o_scr[0, rs, :] + e2 * o_scr[1, rs, :] + e3 * o_scr[2, rs, :]
        o_ref[rs, :] = (num / (e1 + e2 + e3)).astype(o_ref.dtype)
        return carry

    lax.fori_loop(0, span // ct, combine, 0, unroll=2)


def _span_attn(slabs, batch, seq, layer, depth, win_k, win_v):
    span = WIN_MAX
    assert seq % span == 0 and len(DILATED_PATTERNS) == 3
    nsp = seq // span
    slab = lambda s0: pl.BlockSpec((None, span, HD_A), lambda b, h, n: (s0 + h, b * nsp + n, 0))
    win_spec = pl.BlockSpec((None, None, None, span, HD_A), lambda b, h, n: (layer, b, h, 0, 0))
    win_shape = jax.ShapeDtypeStruct((depth, batch, H_A, span, HD_A), F32)
    in_specs = [slab(SLAB_Q), slab(SLAB_K), slab(SLAB_V)]
    args = [slabs, slabs, slabs]
    aliases = {}
    if win_k is not None:
        in_specs += [pl.BlockSpec(memory_space=pl.ANY)] * 2
        args += [win_k, win_v]
        aliases = {3: 1, 4: 2}
    skew = 6
    return pl.pallas_call(
        functools.partial(_span_attn_kernel, skew=skew, n_carried=len(aliases)),
        grid=(batch, H_A, nsp),
        in_specs=in_specs,
        out_specs=[pl.BlockSpec((span, HD_A), lambda b, h, n: (b * nsp + n, h)), win_spec, win_spec],
        out_shape=[jax.ShapeDtypeStruct((slabs.shape[1], D_A), BF16), win_shape, win_shape],
        scratch_shapes=[pltpu.VMEM((2 * span, HD_A), F32), pltpu.VMEM((2 * span, HD_A), F32),
                        pltpu.VMEM((3, span, HD_A), F32), pltpu.VMEM((3, span, HD_A), F32),
                        pltpu.VMEM((2, ATTN_BLK, ATTN_BLK), F32)],
        input_output_aliases=aliases,
        compiler_params=_params("arbitrary", "arbitrary", "arbitrary"),
        name="span_attn",
    )(*args)


def _dec_attn_kernel(q_ref, kn_ref, vn_ref, kc_ref, vc_ref, o_ref, *, past):
    t = q_ref.shape[1]
    scale = HD_A ** -0.5
    nt = (((1,), (1,)), ((), ()))
    d_c = past + lax.broadcasted_iota(jnp.int32, (t, past), 0) - lax.broadcasted_iota(jnp.int32, (t, past), 1)
    d_n = lax.broadcasted_iota(jnp.int32, (t, t), 0) - lax.broadcasted_iota(jnp.int32, (t, t), 1)
    masks = []
    for window, dil in DILATED_PATTERNS:
        m_c = jnp.logical_and((d_c & (dil - 1)) == 0, d_c <= window)
        m_n = jnp.logical_and(jnp.logical_and(d_n >= 0, (d_n & (dil - 1)) == 0), d_n <= window)
        masks.append((m_c, m_n))
    for h in range(H_A):
        q = q_ref[h]
        vc = vc_ref[h]
        vn = vn_ref[h]
        s_c = lax.dot_general(q, kc_ref[h], nt, preferred_element_type=F32) * scale
        s_n = lax.dot_general(q, kn_ref[h], nt, preferred_element_type=F32) * scale
        pcs, pns, dens, lses = [], [], [], []
        for m_c, m_n in masks:
            a_c = jnp.where(m_c, s_c, NEG_INF)
            a_n = jnp.where(m_n, s_n, NEG_INF)
            m = jnp.maximum(jnp.max(a_c, axis=-1, keepdims=True), jnp.max(a_n, axis=-1, keepdims=True))
            p_c = jnp.exp(a_c - m)
            p_n = jnp.exp(a_n - m)
            den = jnp.sum(p_c, axis=-1, keepdims=True) + jnp.sum(p_n, axis=-1, keepdims=True)
            pcs.append(p_c)
            pns.append(p_n)
            dens.append(den)
            lses.append(m + jnp.log(den))
        o_all = jnp.dot(jnp.concatenate(pcs, axis=0), vc, preferred_element_type=F32)
        o_all = o_all + jnp.dot(jnp.concatenate(pns, axis=0), vn, preferred_element_type=F32)
        outs = [o_all[i * t:(i + 1) * t] / dens[i] for i in range(len(masks))]
        mx = jnp.maximum(jnp.maximum(lses[0], lses[1]), lses[2])
        es = [jnp.exp(l - mx) for l in lses]
        num = es[0] * outs[0] + es[1] * outs[1] + es[2] * outs[2]
        o_ref[:, h * HD_A:(h + 1) * HD_A] = (num / (es[0] + es[1] + es[2])).astype(o_ref.dtype)


def _dec_attn(slabs, row0, cache_k, cache_v, layer, batch, t):
    past = cache_k.shape[3]
    assert row0 % t == 0 and SLAB_Q % H_A == 0 and SLAB_K % H_A == 0 and SLAB_V % H_A == 0
    new_spec = lambda s0: pl.BlockSpec((H_A, t, HD_A), lambda b: (s0 // H_A, row0 // t + b, 0))
    cache_spec = pl.BlockSpec((None, None, H_A, past, HD_A), lambda b: (layer, b, 0, 0, 0))
    return pl.pallas_call(
        functools.partial(_dec_attn_kernel, past=past),
        grid=(batch,),
        in_specs=[new_spec(SLAB_Q), new_spec(SLAB_K), new_spec(SLAB_V), cache_spec, cache_spec],
        out_specs=pl.BlockSpec((t, D_A), lambda b: (b, 0)),
        out_shape=jax.ShapeDtypeStruct((batch * t, D_A), F32),
        compiler_params=_params("arbitrary"),
        name="dec_attn",
    )(slabs, slabs, slabs, cache_k, cache_v)


def _pool_kernel(*refs, pos0, has_prev):
    if has_prev:
        u_ref, prev_ref, hist_ref, w_ref, sc_ref, o_ref, ext_ref = refs
    else:
        u_ref, hist_ref, w_ref, sc_ref, o_ref, ext_ref = refs
        prev_ref = None
    i = pl.program_id(1)
    tm = u_ref.shape[1]

    @pl.when(i == 0)
    def _():
        for gi in range(N_POOL):
            ext_ref[gi, 0:HIST_ROWS, :] = hist_ref[0, :, gi * POOL_GROUP:(gi + 1) * POOL_GROUP]

    if has_prev:
        @pl.when(i > 0)
        def _():
            ext_ref[:, 0:HIST_ROWS, :] = prev_ref[...]

    ext_ref[:, HIST_ROWS:HIST_ROWS + tm, :] = u_ref[...]
    pos = pos0 + i * tm + lax.broadcasted_iota(jnp.int32, (tm, POOL_GROUP), 0)
    for gi, w in enumerate(POOL_WINDOWS):
        cs = slice(gi * POOL_GROUP, (gi + 1) * POOL_GROUP)
        tok = ext_ref[gi, HIST_ROWS:HIST_ROWS + tm, :]
        tot = tok
        for back in range(1, w):
            tot = tot + ext_ref[gi, HIST_ROWS - back:HIST_ROWS - back + tm, :]
        cnt = jnp.minimum(w, pos + 1).astype(F32)
        pooled = tot / cnt - tok
        out = jnp.dot(pooled.astype(BF16), w_ref[gi].astype(BF16), preferred_element_type=F32)
        o_ref[:, cs] = (out * sc_ref[:, cs]).astype(o_ref.dtype)


def _pool(slabs, row0, out_rows, hist, w_pool, scale, layer, batch, t, pos0):
    tm = _row_tile(t, 1024)
    nt = t // tm
    has_prev = nt > 1
    assert row0 % tm == 0
    rb = lambda b, i: row0 // tm + b * nt + i
    ob = lambda b, i: b * nt + i
    in_specs = [pl.BlockSpec((N_POOL, tm, LANE), lambda b, i: (SLAB_U // N_POOL, rb(b, i), 0))]
    args = [slabs]
    if has_prev:
        per = tm // HIST_ROWS
        in_specs.append(pl.BlockSpec((N_POOL, HIST_ROWS, LANE),
                                     lambda b, i: (SLAB_U // N_POOL, jnp.maximum(rb(b, i) * per - 1, 0), 0)))
        args.append(slabs)
    in_specs += [
        pl.BlockSpec((1, HIST_ROWS, D_B), lambda b, i: (b, 0, 0)),
        pl.BlockSpec((None, N_POOL, POOL_GROUP, POOL_GROUP), lambda b, i: (layer, 0, 0, 0)),
        pl.BlockSpec((None, 1, D_B), lambda b, i: (layer, 0, 0)),
    ]
    args += [hist, w_pool, scale]
    return pl.pallas_call(
        functools.partial(_pool_kernel, pos0=pos0, has_prev=has_prev),
        grid=(batch, nt),
        in_specs=in_specs,
        out_specs=pl.BlockSpec((tm, D_B), lambda b, i: (ob(b, i), 0)),
        out_shape=jax.ShapeDtypeStruct((out_rows, D_B), BF16 if tm % PACK == 0 else F32),
        scratch_shapes=[pltpu.VMEM((N_POOL, HIST_ROWS + tm, LANE), F32)],
        compiler_params=_params("arbitrary", "arbitrary"),
        name="pool_mix",
    )(*args)


def _gla_kernel(q_ref, k_ref, v_ref, r_ref, ga_ref, w2_ref, bg_ref, gn_ref, s0_ref, o_ref, s_out_ref,
                st_ref, g_ref, *, chunk, group):
    i = pl.program_id(1)
    ts = q_ref.shape[1]

    @pl.when(i == 0)
    def _():
        st_ref[...] = s0_ref[0]

    z = jnp.dot(ga_ref[...], w2_ref[...], preferred_element_type=F32) + bg_ref[...]
    g_ref[...] = (jnp.minimum(z, 0.0) - jnp.log(1.0 + jnp.exp(-jnp.abs(z)))) / GATE_NORMALIZER

    gr = group * chunk
    ri = lax.broadcasted_iota(jnp.int32, (gr, gr), 0)
    ci = lax.broadcasted_iota(jnp.int32, (gr, gr), 1)
    same_chunk = (ri // chunk) == (ci // chunk)
    tril = jnp.logical_and(same_chunk, ci <= ri)
    sum_mat = jnp.concatenate([tril.astype(F32), same_chunk.astype(F32)], axis=0)
    split_sums = gr % PACK == 0
    qscale = DK_C ** -0.5
    nt = (((1,), (1,)), ((), ()))
    tn = (((0,), (0,)), ((), ()))

    def body(gi, carry):
        rows = pl.ds(pl.multiple_of(gi * gr, gr), gr)
        if split_sums:
            rem = g_ref[rows, :]
            sums = None
            for _ in range(3):
                part = rem.astype(BF16)
                rem = rem - part.astype(F32)
                d = jnp.dot(sum_mat.astype(BF16), part, preferred_element_type=F32)
                sums = d if sums is None else sums + d
        else:
            sums = jnp.dot(sum_mat, g_ref[rows, :], preferred_element_type=F32, precision=lax.Precision.HIGHEST)
        b = sums[:gr]
        bl = sums[gr:]
        eb = jnp.exp(b)
        enb = jnp.exp(-b)
        ebl = jnp.exp(bl)
        ekd = jnp.exp(bl - b)
        heads = range(H_C)
        ksl = [slice(h * DK_P, (h + 1) * DK_P) for h in heads]
        vh = [jnp.concatenate([v_ref[VS * h + s, rows, :] for s in range(VS)], axis=-1).astype(BF16) for h in heads]
        qe = [(q_ref[h, rows, :] * qscale * eb[:, ksl[h]]).astype(BF16) for h in heads]
        ke = [(k_ref[h, rows, :] * enb[:, ksl[h]]).astype(BF16) for h in heads]
        kd = [(k_ref[h, rows, :] * ekd[:, ksl[h]]).astype(BF16) for h in heads]
        a = [lax.dot_general(qe[h], ke[h], nt, preferred_element_type=F32) for h in heads]
        st = [st_ref[h] for h in heads]
        inter = [[] for _ in heads]
        for c in range(group):
            cr = slice(c * chunk, (c + 1) * chunk)
            kv = [lax.dot_general(vh[h][cr], kd[h][cr], tn, preferred_element_type=F32) for h in heads]
            for h in heads:
                inter[h].append(lax.dot_general(qe[h][cr], st[h].astype(BF16), nt,
                                                preferred_element_type=F32))
            st = [st[h] * ebl[c * chunk:c * chunk + 1, ksl[h]] + kv[h] for h in heads]
        for h in heads:
            st_ref[h] = st[h]
            am = jnp.where(tril, a[h], 0.0).astype(BF16)
            o = jnp.dot(am, vh[h], preferred_element_type=F32)
            o = o + (inter[h][0] if group == 1 else jnp.concatenate(inter[h], axis=0))
            ms = jnp.sum(o * o, axis=-1, keepdims=True) * (1.0 / DV_C)
            on = o * lax.rsqrt(ms + EPS) * gn_ref[...]
            rh = jnp.concatenate([r_ref[VS * h + s, rows, :] for s in range(VS)], axis=-1)
            o_ref[rows, h * DV_P:(h + 1) * DV_P] = (on * (rh * (1.0 / (1.0 + jnp.exp(-rh))))).astype(o_ref.dtype)
        return carry

    lax.fori_loop(0, ts // gr, body, 0, unroll=2)

    @pl.when(i == pl.num_programs(1) - 1)
    def _():
        s_out_ref[0] = st_ref[...]


def _gla(slabs, row0, out_rows, w2p, bgp, gnp, s0t, layer, batch, t):
    chunk = min(GLA_CHUNK, t)
    out_dtype = BF16 if chunk % PACK == 0 else F32
    ts = _row_tile(t, 512)
    nt = t // ts
    group = GLA_GROUP if ts % (GLA_GROUP * chunk) == 0 else 1
    assert t % chunk == 0 and ts % chunk == 0 and row0 % ts == 0
    qk_w, v_w = H_C * DK_P, H_C * DV_P
    rb = lambda b, i: row0 // ts + b * nt + i
    grp = lambda n, s0: pl.BlockSpec((n, ts, LANE), lambda b, i: (s0 // n, rb(b, i), 0))
    assert SLAB_VC % (H_C * VS) == 0 and SLAB_RC % (H_C * VS) == 0 and SLAB_QC % H_C == 0 and SLAB_KC % H_C == 0
    par = lambda shape: pl.BlockSpec((None,) + shape, lambda b, i: (layer,) + (0,) * len(shape))
    state_spec = pl.BlockSpec((1, H_C, DV_P, DK_P), lambda b, i: (b, 0, 0, 0))
    return pl.pallas_call(
        functools.partial(_gla_kernel, chunk=chunk, group=group),
        grid=(batch, nt),
        in_specs=[grp(H_C, SLAB_QC), grp(H_C, SLAB_KC), grp(H_C * VS, SLAB_VC), grp(H_C * VS, SLAB_RC),
                  pl.BlockSpec((None, ts, LANE), lambda b, i: (SLAB_G, rb(b, i), 0)),
                  par((LANE, qk_w)), par((1, qk_w)), par((1, DV_P)), state_spec],
        out_specs=[pl.BlockSpec((ts, v_w), lambda b, i: (b * nt + i, 0)), state_spec],
        out_shape=[jax.ShapeDtypeStruct((out_rows, v_w), out_dtype),
                   jax.ShapeDtypeStruct((batch, H_C, DV_P, DK_P), F32)],
        scratch_shapes=[pltpu.VMEM((H_C, DV_P, DK_P), F32), pltpu.VMEM((ts, qk_w), F32)],
        compiler_params=_params("arbitrary", "arbitrary"),
        name="gla",
    )(slabs, slabs, slabs, slabs, slabs, w2p, bgp, gnp, s0t)


def _pad_head_rows(w, width, padded):
    depth, _, k = w.shape
    w = w.reshape(depth, H_C, width, k)
    w = jnp.pad(w, ((0, 0), (0, 0), (0, padded - width), (0, 0)))
    return w.reshape(depth, H_C * padded, k)


def _pad_heads(w, width, padded):
    lead = w.shape[:-1]
    w = w.reshape(lead + (H_C, width))
    w = jnp.pad(w, [(0, 0)] * len(lead) + [(0, 0), (0, padded - width)])
    return w.reshape(lead + (H_C * padded,))


def _relayout_weights(w_in, w_gate2, b_gate, gla_norm_g, w_o):
    wt = jnp.swapaxes(w_in, 1, 2)
    c0 = 3 * D_A + D_B
    qk = H_C * DK_C
    wq = wt[:, c0:c0 + qk]
    wk = wt[:, c0 + qk:c0 + 2 * qk]
    wv = wt[:, c0 + 2 * qk:c0 + 2 * qk + D_C]
    wr = wt[:, c0 + 2 * qk + D_C:c0 + 2 * qk + 2 * D_C]
    wg = wt[:, c0 + 2 * qk + 2 * D_C:]
    w_all = jnp.concatenate([
        _pad_head_rows(wv, DV_C, DV_P), _pad_head_rows(wr, DV_C, DV_P),
        _pad_head_rows(wq, DK_C, DK_P), _pad_head_rows(wk, DK_C, DK_P),
        wt[:, 3 * D_A:c0],
        jnp.pad(wg, ((0, 0), (0, 2 * LANE - GATE_RANK), (0, 0))),
        wt[:, :3 * D_A],
    ], axis=1).astype(BF16)
    assert w_all.shape[1] == N_SLABS * LANE
    w2p = jnp.pad(_pad_heads(w_gate2, DK_C, DK_P), ((0, 0), (0, LANE - GATE_RANK), (0, 0)))
    bgp = _pad_heads(b_gate, DK_C, DK_P)[:, None, :]
    gnp = jnp.pad(gla_norm_g, ((0, 0), (0, DV_P - DV_C)))[:, None, :]
    wo_c = _pad_head_rows(w_o[:, D_A + D_B:], DV_C, DV_P)
    return w_all, w2p, bgp, gnp, wo_c


def _state_to_kernel(s):
    s = jnp.swapaxes(s, -1, -2)
    return jnp.pad(s, [(0, 0)] * (s.ndim - 2) + [(0, DV_P - DV_C), (0, DK_P - DK_C)])


def _state_from_kernel(s):
    return jnp.swapaxes(s[..., :DV_C, :DK_C], -1, -2)


def _heads_to_rows(a):
    return jnp.transpose(a, (0, 1, 3, 2, 4))


def _set_rows(big, row0, small):
    return lax.dynamic_update_slice(big, small.astype(big.dtype), (row0, 0))


def kernel(x_prompt, x_sample, cache_win_k, cache_win_v, state_pool, state_gla, norm1_g, w_in, w_gate2, b_gate,
           w_pool, pool_scale, gla_norm_g, w_o, norm2_g, w_ffn_gate, w_ffn_up, w_ffn_down, final_norm_g):
    bp, sp, d_model = x_prompt.shape
    bs, ts, _ = x_sample.shape
    depth = w_in.shape[0]
    d_ff = w_ffn_down.shape[1]
    mp, ms = bp * sp, bs * ts
    m_all = mp + ms
    assert min(WIN_MAX, sp) == WIN_MAX
    assert d_ff % 2 == 0

    w_all, w2p, bgp, gnp, wo_c = _relayout_weights(w_in, w_gate2, b_gate, gla_norm_g, w_o)
    cache_k = jnp.transpose(cache_win_k, (0, 1, 3, 2, 4))
    cache_v = jnp.transpose(cache_win_v, (0, 1, 3, 2, 4))
    pool_scale3 = pool_scale[:, None, :]

    x = jnp.concatenate([x_prompt.reshape(mp, d_model), x_sample.reshape(ms, d_model)], axis=0)
    zero_hist = jnp.zeros((bp, HIST_ROWS, D_B), F32)
    zero_state = jnp.zeros((bp, H_C, DV_P, DK_P), F32)
    hist_all = jnp.pad(state_pool, ((0, 0), (0, 0), (HIST_ROWS - POOL_BUF, 0), (0, 0)))
    state_all = _state_to_kernel(state_gla)
    p_pool, p_gla, s_k, s_v, s_pool, s_gla = [], [], [], [], [], []
    win_k = win_v = None
    xg, ss = _norm_prep(x, norm1_g[0])
    for l in range(depth):
        slabs = _proj(xg, ss, w_all, l)

        out_a, win_k, win_v = _span_attn(slabs, bp, sp, l, depth, win_k, win_v)
        out_b = _pool(slabs, 0, m_all, zero_hist, w_pool, pool_scale3, l, bp, sp, 0)
        out_c, st_p = _gla(slabs, 0, m_all, w2p, bgp, gnp, zero_state, l, bp, sp)
        u_p = jnp.stack([slabs[SLAB_U:SLAB_U + N_POOL, (b + 1) * sp - POOL_BUF:(b + 1) * sp] for b in range(bp)])
        p_pool.append(jnp.transpose(u_p, (0, 2, 1, 3)).reshape(bp, POOL_BUF, D_B))
        p_gla.append(st_p)

        sa = _dec_attn(slabs, mp, cache_k, cache_v, l, bs, ts)
        sb = _pool(slabs, mp, ms, hist_all[l], w_pool, pool_scale3, l, bs, ts, PAST_LEN)
        sc, st_s = _gla(slabs, mp, ms, w2p, bgp, gnp, state_all[l], l, bs, ts)
        out_a = _set_rows(out_a, mp, sa)
        out_b = _set_rows(out_b, mp, sb)
        out_c = _set_rows(out_c, mp, sc)
        new_kv = slabs[SLAB_K:SLAB_K + 2 * H_A, mp:].reshape(2, H_A, bs, ts, HD_A)
        s_k.append(jnp.swapaxes(new_kv[0], 0, 1))
        s_v.append(jnp.swapaxes(new_kv[1], 0, 1))
        u_s = jnp.transpose(slabs[SLAB_U:SLAB_U + N_POOL, mp:].reshape(N_POOL, bs, ts, LANE), (1, 2, 0, 3))
        u_ext = jnp.concatenate([state_pool[l], u_s.reshape(bs, ts, D_B)], axis=1)
        s_pool.append(u_ext[:, -POOL_BUF:])
        s_gla.append(st_s)

        x, xg, ss = _matmul_res([(w_o, 0, [(out_a, D_A, 0), (out_b, D_B, 0)]),
                                 (wo_c, 0, [(out_c, H_C * DV_P, 0)])], l, x, "out_proj", norm_gain=norm2_g[l])
        f = _glu(xg, ss, w_ffn_gate, w_ffn_up, l)
        half = d_ff // 2
        x = _matmul_res([(w_ffn_down, 0, [(f, half, 0)])], l, x, "ffn_down")
        if l + 1 < depth:
            x, xg, ss = _matmul_res([(w_ffn_down, 1, [(f, half, 1)])], l, x, "ffn_down", norm_gain=norm1_g[l + 1])
        else:
            x = _matmul_res([(w_ffn_down, 1, [(f, half, 1)])], l, x, "ffn_down")

    y_prompt = _rmsnorm(x, final_norm_g, F32, row0=0, rows=mp).reshape(bp, sp, d_model)
    y_sample = _rmsnorm(x, final_norm_g, F32, row0=mp, rows=ms).reshape(bs, ts, d_model)
    return (y_prompt, y_sample, _heads_to_rows(win_k), _heads_to_rows(win_v),
            jnp.stack(p_pool), _state_from_kernel(jnp.stack(p_gla)), _heads_to_rows(jnp.stack(s_k)),
            _heads_to_rows(jnp.stack(s_v)), jnp.stack(s_pool), _state_from_kernel(jnp.stack(s_gla)))
```
